```python
import math
import jax, jax.numpy as jnp
from jax import lax
import numpy as np

D_MODEL = 1024
BATCH = 4
SEQ = 4096
DEPTH = 2

CHUNK = 64
QBLK = 128
ATT_HEADS = 16
ATT_HEAD_DIM = 64
ATT_WIDTH = ATT_HEADS * ATT_HEAD_DIM
POOL_WINDOWS = (2, 4, 8, 16)
POOL_GROUPS = len(POOL_WINDOWS)
POOL_WIDTH = D_MODEL
POOL_GROUP_DIM = POOL_WIDTH // POOL_GROUPS
SSM_EXPAND = 2
SSM_INNER = SSM_EXPAND * D_MODEL
SSM_HEAD_DIM = 64
SSM_HEADS = SSM_INNER // SSM_HEAD_DIM
SSM_GROUPS = 4
SSM_HEADS_PER_GROUP = SSM_HEADS // SSM_GROUPS
SSM_STATE = 128
SSM_CONV = 4
SSM_CONV_DIM = SSM_INNER + 2 * SSM_GROUPS * SSM_STATE
N_BRANCHES = 3
N_EXPERTS = 16
N_EXPERT_GROUPS = 4
EXPERTS_PER_GROUP = N_EXPERTS // N_EXPERT_GROUPS
TOP_K = 2
EXPERT_DFF = 512
MOE_BLOCK = 256
DEEPNORM_ALPHA = (2 * DEPTH) ** 0.25
DEEPNORM_BETA = (8 * DEPTH) ** -0.25
LN_EPS = 1e-5
RMS_EPS = 1e-6
IN_SPLITS = (ATT_WIDTH, ATT_WIDTH, ATT_WIDTH, ATT_HEADS, POOL_WIDTH,
             SSM_INNER, SSM_CONV_DIM, SSM_HEADS, N_BRANCHES * D_MODEL)
IN_WIDTH = sum(IN_SPLITS)

kernel_name = 'hybrid_fox_pool_ssd_grouped_moe_deepnorm'


def layer_norm(x, g=None, b=None):
    x32 = x.astype(jnp.float32)
    mu = jnp.mean(x32, axis=-1, keepdims=True)
    xc = x32 - mu
    y = xc * lax.rsqrt(jnp.mean(xc * xc, axis=-1, keepdims=True) + LN_EPS)
    if g is not None:
        y = y * g + b
    return y.astype(x.dtype)


def split_columns(proj):
    points, acc = [], 0
    for w in IN_SPLITS[:-1]:
        acc += w
        points.append(acc)
    return jnp.split(proj, points, axis=-1)


def forgetting_attention(q, k, v, log_f):
    S = q.shape[1]
    scale = ATT_HEAD_DIM ** -0.5
    F = jnp.transpose(jnp.cumsum(log_f, axis=1), (0, 2, 1))
    q, k, v = (jnp.transpose(t, (0, 2, 1, 3)) for t in (q, k, v))
    outs = []
    for blk in range(S // QBLK):
        s0, s1 = blk * QBLK, (blk + 1) * QBLK
        logits = jnp.einsum('bhqd,bhkd->bhqk', q[:, :, s0:s1], k[:, :, :s1]).astype(jnp.float32) * scale
        logits = logits + F[:, :, s0:s1, None] - F[:, :, None, :s1]
        mask = jnp.arange(s0, s1)[:, None] >= jnp.arange(s1)[None, :]
        p = jax.nn.softmax(jnp.where(mask, logits, -jnp.inf), axis=-1)
        outs.append(jnp.einsum('bhqk,bhkd->bhqd', p.astype(v.dtype), v[:, :, :s1]))
    o = jnp.concatenate(outs, axis=2)
    B_ = o.shape[0]
    return jnp.transpose(o, (0, 2, 1, 3)).reshape(B_, S, ATT_WIDTH)


def multiscale_pool(u, w_pool, pool_scale):
    Bsz, S, _ = u.shape
    u32 = u.astype(jnp.float32).reshape(Bsz, S, POOL_GROUPS, POOL_GROUP_DIM)
    cs = jnp.cumsum(u32, axis=1)
    pos = jnp.arange(1, S + 1, dtype=jnp.float32)[:, None]
    pooled = []
    for g, w in enumerate(POOL_WINDOWS):
        csg = cs[:, :, g]
        lagged = jnp.pad(csg, ((0, 0), (w, 0), (0, 0)))[:, :S]
        pooled.append((csg - lagged) / jnp.minimum(pos, float(w)) - u32[:, :, g])
    pooled = jnp.stack(pooled, axis=2)
    y = jnp.einsum('bsgc,gcd->bsgd', pooled, w_pool.astype(jnp.float32)).reshape(Bsz, S, POOL_WIDTH)
    return (y * pool_scale).astype(u.dtype)


def causal_depthwise_conv(u, w, b):
    out = lax.conv_general_dilated(
        u, w[:, None, :].astype(u.dtype), window_strides=(1,),
        padding=[(SSM_CONV - 1, 0)], dimension_numbers=('NWC', 'WIO', 'NWC'),
        feature_group_count=u.shape[-1])
    return out + b


def ssd_mixer(z, xbc, dt_raw, conv_w, conv_b, dt_bias, a_log, d_skip, norm_w, w_o):
    Bsz, S, _ = z.shape
    nc = S // CHUNK
    G, R, P, N, L = SSM_GROUPS, SSM_HEADS_PER_GROUP, SSM_HEAD_DIM, SSM_STATE, CHUNK
    f32 = jnp.float32
    xbc = jax.nn.silu(causal_depthwise_conv(xbc, conv_w, conv_b))
    xs, bm, cm = jnp.split(xbc, [SSM_INNER, SSM_INNER + G * N], axis=-1)
    xs = xs.astype(f32).reshape(Bsz, nc, L, G, R, P)
    bm = bm.astype(f32).reshape(Bsz, nc, L, G, N)
    cm = cm.astype(f32).reshape(Bsz, nc, L, G, N)
    dt = jax.nn.softplus(dt_raw.astype(f32) + dt_bias).reshape(Bsz, nc, L, G, R)
    a = -jnp.exp(a_log.astype(f32)).reshape(G, R)
    a_cs = jnp.cumsum(dt * a, axis=2)
    causal = jnp.tril(jnp.ones((L, L), bool))[:, :, None, None]
    seg = a_cs[:, :, :, None] - a_cs[:, :, None, :]
    decay = jnp.where(causal, jnp.exp(jnp.where(causal, seg, 0.0)), 0.0)
    cb = jnp.einsum('bclgn,bcsgn->bclsg', cm, bm)
    scores = cb[..., None] * decay * dt[:, :, None]
    y_diag = jnp.einsum('bclsgr,bcsgrp->bclgrp', scores, xs)
    decay_end = jnp.exp(a_cs[:, :, -1:] - a_cs)
    states = jnp.einsum('bcsgn,bcsgr,bcsgrp->bcgrpn', bm, decay_end * dt, xs)
    chunk_decay = jnp.exp(a_cs[:, :, -1])

    def step(h, inp):
        st, dec = inp
        return dec[..., None, None] * h + st, h

    h0 = jnp.zeros((Bsz, G, R, P, N), states.dtype)
    _, h_prev = lax.scan(step, h0, (jnp.moveaxis(states, 1, 0), jnp.moveaxis(chunk_decay, 1, 0)))
    h_prev = jnp.moveaxis(h_prev, 0, 1)
    y_off = jnp.einsum('bclgn,bcgrpn->bclgrp', cm, h_prev) * jnp.exp(a_cs)[..., None]
    y = y_diag + y_off + d_skip.astype(f32).reshape(G, R)[:, :, None] * xs
    y = y.reshape(Bsz, S, SSM_INNER) * jax.nn.silu(z.astype(f32))
    yg = y.reshape(Bsz, S, G, SSM_INNER // G)
    yg = yg * lax.rsqrt(jnp.mean(yg * yg, axis=-1, keepdims=True) + RMS_EPS)
    y = yg.reshape(Bsz, S, SSM_INNER) * norm_w
    return y.astype(z.dtype) @ w_o


def hybrid_mixer(h, w_in, b_forget, w_attn_o, w_pool, pool_scale, conv_w, conv_b,
                 dt_bias, a_log, d_skip, ssm_norm_w, w_ssm_o, w_out):
    Bsz, S, _ = h.shape
    proj = h @ w_in
    q, k, v, f_raw, u_pool, z, xbc, dt_raw, gate_raw = split_columns(proj)
    log_f = jax.nn.log_sigmoid(f_raw.astype(jnp.float32) + b_forget)
    hs = (Bsz, S, ATT_HEADS, ATT_HEAD_DIM)
    y_att = forgetting_attention(q.reshape(hs), k.reshape(hs), v.reshape(hs), log_f) @ w_attn_o
    y_pool = multiscale_pool(u_pool, w_pool, pool_scale)
    y_ssm = ssd_mixer(z, xbc, dt_raw, conv_w, conv_b, dt_bias, a_log, d_skip, ssm_norm_w, w_ssm_o)
    gates = jax.nn.sigmoid(gate_raw.astype(jnp.float32)).reshape(Bsz, S, N_BRANCHES, D_MODEL)
    merged = gates[:, :, 0] * y_att + gates[:, :, 1] * y_pool + gates[:, :, 2] * y_ssm
    return merged.astype(h.dtype) @ w_out


def grouped_moe(h, w_router, b_router, w1, w3, w2):
    Bsz, S, D = h.shape
    N = Bsz * S
    NK = N * TOP_K
    ht = h.reshape(N, D)
    probs = jax.nn.softmax((ht @ w_router).astype(jnp.float32), axis=-1)
    sel = probs + b_router
    group_score = lax.top_k(sel.reshape(N, N_EXPERT_GROUPS, EXPERTS_PER_GROUP), TOP_K)[0].sum(-1)
    g_idx = jnp.argmax(group_score, axis=-1)
    in_group = (jnp.arange(N_EXPERTS) // EXPERTS_PER_GROUP)[None, :] == g_idx[:, None]
    _, e_idx = lax.top_k(jnp.where(in_group, sel, -jnp.inf), TOP_K)
    gate_w = jnp.take_along_axis(probs, e_idx, axis=1)
    gate_w = gate_w / jnp.sum(gate_w, axis=-1, keepdims=True)
    e_flat = e_idx.reshape(NK)
    tok_flat = jnp.repeat(jnp.arange(N, dtype=jnp.int32), TOP_K)
    w_flat = gate_w.reshape(NK)
    order = jnp.argsort(e_flat)
    e_sorted, tok_sorted, w_sorted = e_flat[order], tok_flat[order], w_flat[order]
    counts = jnp.bincount(e_flat, length=N_EXPERTS)
    starts = jnp.cumsum(counts) - counts
    padded = ((counts + MOE_BLOCK - 1) // MOE_BLOCK) * MOE_BLOCK
    pends = jnp.cumsum(padded)
    pstarts = pends - padded
    dest = pstarts[e_sorted] + (jnp.arange(NK) - starts[e_sorted])
    P = NK + N_EXPERTS * MOE_BLOCK
    nblk = P // MOE_BLOCK
    slot_tok = jnp.full((P,), N, jnp.int32).at[dest].set(tok_sorted)
    slot_w = jnp.zeros((P,), jnp.float32).at[dest].set(w_sorted)
    blk_expert = jnp.minimum(
        jnp.searchsorted(pends, jnp.arange(nblk) * MOE_BLOCK, side='right'), N_EXPERTS - 1)
    x_pad = jnp.concatenate([ht, jnp.zeros((1, D), ht.dtype)], axis=0)
    xs = x_pad[slot_tok].reshape(nblk, MOE_BLOCK, D)

    def expert_block(args):
        xb, e = args
        return (jax.nn.silu(xb @ w1[e]) * (xb @ w3[e])) @ w2[e]

    ys = lax.map(expert_block, (xs, blk_expert)).reshape(P, D) * slot_w[:, None]
    out = jax.ops.segment_sum(ys, slot_tok, num_segments=N + 1)[:N]
    return out.reshape(Bsz, S, D).astype(h.dtype)


def setup_inputs(seed: int = 0) -> dict:
    key = jax.random.key(seed)
    ks = jax.random.split(key, 26)

    def nrm(k, shape, scale):
        return jax.random.normal(k, shape, jnp.float32) * scale

    col_scale = jnp.concatenate([
        jnp.ones((2 * ATT_WIDTH,), jnp.float32),
        jnp.full((ATT_WIDTH,), DEEPNORM_BETA, jnp.float32),
        jnp.ones((IN_WIDTH - 3 * ATT_WIDTH,), jnp.float32)])
    dt0 = jnp.exp(jax.random.uniform(ks[9], (DEPTH, SSM_HEADS), jnp.float32,
                                     minval=math.log(1e-3), maxval=math.log(1e-1)))
    return {
        'x': nrm(ks[0], (BATCH, SEQ, D_MODEL), 1.0),
        'c': nrm(ks[1], (BATCH, D_MODEL), 1.0),
        'w_in': nrm(ks[2], (DEPTH, D_MODEL, IN_WIDTH), D_MODEL ** -0.5) * col_scale,
        'b_forget': 4.0 + nrm(ks[3], (DEPTH, ATT_HEADS), 0.5),
        'w_attn_o': nrm(ks[4], (DEPTH, ATT_WIDTH, D_MODEL), DEEPNORM_BETA * ATT_WIDTH ** -0.5),
        'w_pool': nrm(ks[5], (DEPTH, POOL_GROUPS, POOL_GROUP_DIM, POOL_GROUP_DIM), DEEPNORM_BETA * POOL_GROUP_DIM ** -0.5),
        'pool_scale': 1.0 + nrm(ks[6], (DEPTH, POOL_WIDTH), 0.1),
        'conv_w': nrm(ks[7], (DEPTH, SSM_CONV, SSM_CONV_DIM), SSM_CONV ** -0.5),
        'conv_b': nrm(ks[8], (DEPTH, SSM_CONV_DIM), 0.02),
        'dt_bias': dt0 + jnp.log(-jnp.expm1(-dt0)),
        'a_log': jnp.log(jax.random.uniform(ks[10], (DEPTH, SSM_HEADS), jnp.float32, minval=1.0, maxval=16.0)),
        'd_skip': 1.0 + nrm(ks[11], (DEPTH, SSM_HEADS), 0.1),
        'ssm_norm_w': 1.0 + nrm(ks[12], (DEPTH, SSM_INNER), 0.1),
        'w_ssm_o': nrm(ks[13], (DEPTH, SSM_INNER, D_MODEL), DEEPNORM_BETA * SSM_INNER ** -0.5),
        'w_out': nrm(ks[14], (DEPTH, D_MODEL, D_MODEL), DEEPNORM_BETA * D_MODEL ** -0.5),
        'w_ada': nrm(ks[15], (DEPTH, D_MODEL, 6 * D_MODEL), D_MODEL ** -0.5),
        'b_ada': nrm(ks[16], (DEPTH, 6 * D_MODEL), 0.02),
        'ln_mix_g': 1.0 + nrm(ks[17], (DEPTH, D_MODEL), 0.1),
        'ln_mix_b': nrm(ks[18], (DEPTH, D_MODEL), 0.02),
        'ln_ffn_g': 1.0 + nrm(ks[19], (DEPTH, D_MODEL), 0.1),
        'ln_ffn_b': nrm(ks[20], (DEPTH, D_MODEL), 0.02),
        'w_router': nrm(ks[21], (D_MODEL, N_EXPERTS), D_MODEL ** -0.5),
        'b_router': nrm(ks[22], (N_EXPERTS,), 0.01),
        'w_exp_gate': nrm(ks[23], (DEPTH, N_EXPERTS, D_MODEL, EXPERT_DFF), DEEPNORM_BETA * D_MODEL ** -0.5),
        'w_exp_up': nrm(ks[24], (DEPTH, N_EXPERTS, D_MODEL, EXPERT_DFF), DEEPNORM_BETA * D_MODEL ** -0.5),
        'w_exp_down': nrm(ks[25], (DEPTH, N_EXPERTS, EXPERT_DFF, D_MODEL), DEEPNORM_BETA * EXPERT_DFF ** -0.5),
    }


def reference(x, c, w_in, b_forget, w_attn_o, w_pool, pool_scale, conv_w, conv_b,
              dt_bias, a_log, d_skip, ssm_norm_w, w_ssm_o, w_out, w_ada, b_ada,
              ln_mix_g, ln_mix_b, ln_ffn_g, ln_ffn_b, w_router, b_router,
              w_exp_gate, w_exp_up, w_exp_down):
    cond = jax.nn.silu(c)
    for l in range(DEPTH):
        mod = cond @ w_ada[l] + b_ada[l]
        sh_m, sc_m, g_m, sh_f, sc_f, g_f = jnp.split(mod[:, None, :], 6, axis=-1)
        h = layer_norm(x) * (1.0 + sc_m) + sh_m
        mix = hybrid_mixer(h, w_in[l], b_forget[l], w_attn_o[l], w_pool[l], pool_scale[l],
                           conv_w[l], conv_b[l], dt_bias[l], a_log[l], d_skip[l],
                           ssm_norm_w[l], w_ssm_o[l], w_out[l])
        x = layer_norm(DEEPNORM_ALPHA * x + g_m * mix, ln_mix_g[l], ln_mix_b[l])
        h = layer_norm(x) * (1.0 + sc_f) + sh_f
        ffn = grouped_moe(h, w_router, b_router, w_exp_gate[l], w_exp_up[l], w_exp_down[l])
        x = layer_norm(DEEPNORM_ALPHA * x + g_f * ffn, ln_ffn_g[l], ln_ffn_b[l])
    return x
```

```python
import functools

import jax
import jax.numpy as jnp
from jax import lax
from jax.experimental import pallas as pl
from jax.experimental.pallas import tpu as pltpu

F32 = jnp.float32
BF16 = jnp.bfloat16

D_MODEL = 1024
DEPTH = 2
ATT_HEADS = 16
ATT_HEAD_DIM = 64
ATT_WIDTH = ATT_HEADS * ATT_HEAD_DIM
POOL_WINDOWS = (2, 4, 8, 16)
POOL_GROUPS = len(POOL_WINDOWS)
POOL_WIDTH = D_MODEL
POOL_GROUP_DIM = POOL_WIDTH // POOL_GROUPS
SSM_INNER = 2 * D_MODEL
SSM_HEAD_DIM = 64
SSM_HEADS = SSM_INNER // SSM_HEAD_DIM
SSM_GROUPS = 4
SSM_STATE = 128
SSM_CONV = 4
SSM_CONV_DIM = SSM_INNER + 2 * SSM_GROUPS * SSM_STATE
SSM_GROUP_WIDTH = SSM_INNER // SSM_GROUPS
N_BRANCHES = 3
N_EXPERTS = 16
N_EXPERT_GROUPS = 4
EXPERTS_PER_GROUP = N_EXPERTS // N_EXPERT_GROUPS
TOP_K = 2
EXPERT_DFF = 512
MOE_BLOCK = 256
DEEPNORM_ALPHA = (2 * DEPTH) ** 0.25
LN_EPS = 1e-5
RMS_EPS = 1e-6

LANES = 128
SSD_CHUNK = 64
SMALL_W = LANES
F_COL0 = SSM_HEADS
COL_Q, COL_K, COL_V = 0, ATT_WIDTH, 2 * ATT_WIDTH
COL_POOL = 3 * ATT_WIDTH
COL_Z = COL_POOL + POOL_WIDTH
COL_XBC = COL_Z + SSM_INNER
COL_GATE = COL_XBC + SSM_CONV_DIM
BIG_W = COL_GATE + N_BRANCHES * D_MODEL
VMEM_LIMIT = 48 * 1024 * 1024


def _cparams(sem):
    return pltpu.CompilerParams(dimension_semantics=sem, vmem_limit_bytes=VMEM_LIMIT)


def _silu(v):
    return v * jax.nn.sigmoid(v)


def _softplus(v):
    return jnp.maximum(v, 0.0) + jnp.log1p(jnp.exp(-jnp.abs(v)))


def _log_sigmoid(v):
    return jnp.minimum(v, 0.0) - jnp.log1p(jnp.exp(-jnp.abs(v)))


def _layer_norm(v):
    mu = jnp.mean(v, axis=-1, keepdims=True)
    vc = v - mu
    return vc * lax.rsqrt(jnp.mean(vc * vc, axis=-1, keepdims=True) + LN_EPS)


def _split_bf16(v):
    hi = v.astype(BF16)
    lo = (v - hi.astype(F32)).astype(BF16)
    return hi, lo


def _dot(a, b):
    return jnp.dot(a, b, preferred_element_type=F32)


def _dot3(a_hi, a_lo, b_hi, b_lo):
    return _dot(a_hi, b_hi) + (_dot(a_lo, b_hi) + _dot(a_hi, b_lo))


def _cumsum_rows(v, rows):
    t = v.shape[0]
    k = 1
    while k < t:
        v = v + jnp.where(rows >= k, pltpu.roll(v, k, 0), 0.0)
        k *= 2
    return v


def _ada_kernel(c_ref, w_ref, b_ref, o_ref):
    cond = _silu(c_ref[...])
    c_hi, c_lo = _split_bf16(cond)
    w_hi, w_lo = _split_bf16(w_ref[0])
    o_ref[0] = _dot3(c_hi, c_lo, w_hi, w_lo) + b_ref[0]


def _ada_call(c, w_ada, b_ada):
    depth, d, width = w_ada.shape
    bsz = c.shape[0]
    rows = 8
    c_pad = jnp.zeros((rows, d), F32).at[:bsz].set(c)
    tn = 1024
    out = pl.pallas_call(
        _ada_kernel,
        grid=(depth, width // tn),
        in_specs=[
            pl.BlockSpec((rows, d), lambda l, j: (0, 0)),
            pl.BlockSpec((1, d, tn), lambda l, j: (l, 0, j)),
            pl.BlockSpec((1, 1, tn), lambda l, j: (l, 0, j)),
        ],
        out_specs=pl.BlockSpec((1, rows, tn), lambda l, j: (l, 0, j)),
        out_shape=jax.ShapeDtypeStruct((depth, rows, width), F32),
        compiler_params=_cparams(("arbitrary", "arbitrary")),
        name="ada_mod",
    )(c_pad, w_ada, b_ada.reshape(depth, 1, width))
    return out[:, :bsz]


def _inproj_kernel(x_ref, sh_ref, sc_ref, w_ref, wsh_ref, wsl_ref, big_ref, small_ref, h_ref):
    @pl.when(pl.program_id(1) == 0)
    def _():
        h = _layer_norm(x_ref[...]) * (1.0 + sc_ref[0]) + sh_ref[0]
        h_hi, h_lo = _split_bf16(h)
        h_ref[...] = h_hi
        small_ref[...] = _dot3(h_hi, h_lo, wsh_ref[...], wsl_ref[...])

    big_ref[...] = _dot(h_ref[...], w_ref[...]).astype(BF16)


def _inproj_call(x2, shift, scale, w_big, ws_hi, ws_lo, seq):
    n, d = x2.shape
    tm, tn = 512, 1024
    tiles_per_seq = seq // tm
    return pl.pallas_call(
        _inproj_kernel,
        grid=(n // tm, BIG_W // tn),
        in_specs=[
            pl.BlockSpec((tm, d), lambda i, j: (i, 0)),
            pl.BlockSpec((1, 1, d), lambda i, j: (i // tiles_per_seq, 0, 0)),
            pl.BlockSpec((1, 1, d), lambda i, j: (i // tiles_per_seq, 0, 0)),
            pl.BlockSpec((d, tn), lambda i, j: (0, j)),
            pl.BlockSpec((d, SMALL_W), lambda i, j: (0, 0)),
            pl.BlockSpec((d, SMALL_W), lambda i, j: (0, 0)),
        ],
        out_specs=[
            pl.BlockSpec((tm, tn), lambda i, j: (i, j)),
            pl.BlockSpec((tm, SMALL_W), lambda i, j: (i, 0)),
        ],
        out_shape=[
            jax.ShapeDtypeStruct((n, BIG_W), BF16),
            jax.ShapeDtypeStruct((n, SMALL_W), F32),
        ],
        scratch_shapes=[pltpu.VMEM((tm, d), BF16)],
        compiler_params=_cparams(("arbitrary", "arbitrary")),
        name="inproj",
    )(x2, shift, scale, w_big, ws_hi, ws_lo)


def _fcum_kernel(s_ref, bf_ref, o_ref, carry_ref):
    @pl.when(pl.program_id(1) == 0)
    def _():
        carry_ref[...] = jnp.zeros_like(carry_ref)

    lf = _log_sigmoid(s_ref[...] + bf_ref[...])
    rows = lax.broadcasted_iota(jnp.int32, lf.shape, 0)
    out = _cumsum_rows(lf, rows) + carry_ref[0:1, :]
    o_ref[...] = out
    carry_ref[0:1, :] = out[lf.shape[0] - 1:, :]


def _fcum_call(small, bf_row, bsz, seq):
    t = 512
    per = seq // t
    return pl.pallas_call(
        _fcum_kernel,
        grid=(bsz, per),
        in_specs=[
            pl.BlockSpec((t, SMALL_W), lambda b, i: (b * per + i, 0)),
            pl.BlockSpec((1, SMALL_W), lambda b, i: (0, 0)),
        ],
        out_specs=pl.BlockSpec((t, SMALL_W), lambda b, i: (b * per + i, 0)),
        out_shape=jax.ShapeDtypeStruct(small.shape, F32),
        scratch_shapes=[pltpu.VMEM((8, SMALL_W), F32)],
        compiler_params=_cparams(("arbitrary", "arbitrary")),
        name="forget_cumsum",
    )(small, bf_row)


def _attn_kernel(q_ref, k_ref, v_ref, fq_ref, fk_ref, o_ref, *, tq):
    hp = pl.program_id(1)
    qi = pl.program_id(2)
    dh = ATT_HEAD_DIM
    lane = lax.broadcasted_iota(jnp.int32, (tq, SMALL_W), 1)
    r_iota = lax.broadcasted_iota(jnp.int32, (tq, tq), 0)
    c_iota = lax.broadcasted_iota(jnp.int32, (tq, tq), 1)
    fq_tile = fq_ref[...]
    outs = []
    for hh in range(2):
        head = 2 * hp + hh
        q = q_ref[:, hh * dh:(hh + 1) * dh] * 0.125
        fq = jnp.sum(jnp.where(lane == F_COL0 + head, fq_tile, 0.0), axis=-1, keepdims=True)

        def logits(j, q=q, fq=fq, hh=hh):
            off = pl.multiple_of(j * tq, tq)
            kj = k_ref[pl.ds(off, tq), hh * dh:(hh + 1) * dh]
            vj = v_ref[pl.ds(off, tq), hh * dh:(hh + 1) * dh]
            s = lax.dot_general(q, kj, (((1,), (1,)), ((), ())), preferred_element_type=F32)
            return s + (fq - fk_ref[hh, :, pl.ds(off, tq)]), vj

        s, vj = logits(qi)
        s = jnp.where(r_iota >= c_iota, s, -jnp.inf)
        m0 = jnp.max(s, axis=-1, keepdims=True)
        p = jnp.exp(s - m0)
        l0 = jnp.sum(p, axis=-1, keepdims=True)
        acc0 = _dot(p.astype(BF16), vj)

        def body(j, carry, logits=logits):
            m, l, acc = carry
            s, vj = logits(j)
            m_new = jnp.maximum(m, jnp.max(s, axis=-1, keepdims=True))
            p = jnp.exp(s - m_new)
            a = jnp.exp(m - m_new)
            l = a * l + jnp.sum(p, axis=-1, keepdims=True)
            acc = a * acc + _dot(p.astype(BF16), vj)
            return m_new, l, acc

        _, l, acc = lax.fori_loop(0, qi, body, (m0, l0, acc0))
        outs.append(acc / l)
    o_ref[...] = jnp.concatenate(outs, axis=-1).astype(BF16)


def _attn_call(big3, fcol, frow):
    bsz, seq, _ = big3.shape
    tq = 256
    nq = seq // tq
    pairs = ATT_HEADS // 2
    kb, vb = COL_K // LANES, COL_V // LANES
    return pl.pallas_call(
        functools.partial(_attn_kernel, tq=tq),
        grid=(bsz, pairs, nq),
        in_specs=[
            pl.BlockSpec((None, tq, LANES), lambda b, h, i: (b, i, h)),
            pl.BlockSpec((None, seq, LANES), lambda b, h, i: (b, 0, kb + h)),
            pl.BlockSpec((None, seq, LANES), lambda b, h, i: (b, 0, vb + h)),
            pl.BlockSpec((tq, SMALL_W), lambda b, h, i: (b * nq + i, 0)),
            pl.BlockSpec((2, 1, seq), lambda b, h, i: (b * pairs + h, 0, 0)),
        ],
        out_specs=pl.BlockSpec((None, tq, LANES), lambda b, h, i: (b, i, h)),
        out_shape=jax.ShapeDtypeStruct((bsz, seq, ATT_WIDTH), BF16),
        compiler_params=_cparams(("arbitrary", "arbitrary", "arbitrary")),
        name="fox_attention",
    )(big3, big3, big3, fcol, frow)


def _ssd_kernel(z_ref, xbc_ref, dt_ref, cw_ref, cb_ref, dtb_ref, alog_ref, dsk_ref, nw_ref,
                y_ref, ext_ref, state_ref):
    L = SSD_CHUNK
    P = SSM_HEAD_DIM
    GW = SSM_GROUP_WIDTH
    NS = SSM_STATE
    halo = 8

    @pl.when(pl.program_id(1) == 0)
    def _():
        ext_ref[0:halo, :] = jnp.zeros((halo, SSM_CONV_DIM), F32)
        state_ref[...] = jnp.zeros_like(state_ref)

    ext_ref[halo:halo + L, :] = xbc_ref[...].astype(F32)
    conv = cb_ref[...]
    for j in range(SSM_CONV):
        lo = halo - (SSM_CONV - 1) + j
        conv = conv + cw_ref[j:j + 1, :] * ext_ref[lo:lo + L, :]
    ext_ref[0:halo, :] = ext_ref[L:L + halo, :]
    xc = _silu(conv)

    rows = lax.broadcasted_iota(jnp.int32, (L, LANES), 0)
    lane = lax.broadcasted_iota(jnp.int32, (L, LANES), 1)
    lo_half = lane < P
    dt = _softplus(dt_ref[...] + dtb_ref[...])
    a_cs = _cumsum_rows(dt * (-jnp.exp(alog_ref[...])), rows)
    a_last = a_cs[L - 1:L, :]
    exp_a = jnp.exp(a_cs)
    w_s = jnp.exp(a_last - a_cs) * dt
    cdec = jnp.exp(a_last)
    a_t = jnp.concatenate([a_cs, a_cs], axis=0).T
    dt_t = jnp.concatenate([dt, dt], axis=0).T
    causal = rows >= jnp.where(lo_half, lane, lane - P)
    blk = (lax.broadcasted_iota(jnp.int32, (2 * L, LANES), 0) < L) == \
          (lax.broadcasted_iota(jnp.int32, (2 * L, LANES), 1) < P)

    def pair_cols(mat, h0):
        r = mat.shape[0]
        c0 = jnp.broadcast_to(mat[:, h0:h0 + 1], (r, LANES))
        c1 = jnp.broadcast_to(mat[:, h0 + 1:h0 + 2], (r, LANES))
        return jnp.where(lo_half[:r], c0, c1)

    def pair_rows(mat_t, h0):
        return jnp.where(lo_half[:1], mat_t[h0:h0 + 1, :], mat_t[h0 + 1:h0 + 2, :])

    for g in range(SSM_GROUPS):
        bm = xc[:, SSM_INNER + g * NS:SSM_INNER + (g + 1) * NS]
        cm = xc[:, SSM_INNER + SSM_GROUPS * NS + g * NS:SSM_INNER + SSM_GROUPS * NS + (g + 1) * NS]
        cm_b = cm.astype(BF16)
        bm_t2 = jnp.concatenate([bm, bm], axis=0).T.astype(BF16)
        cb2 = _dot(cm_b, bm_t2)
        st_prev = state_ref[g]
        y_off = _dot(cm_b, st_prev.astype(BF16))
        y_parts, xw_parts, cd_parts = [], [], []
        for pr in range(GW // LANES):
            h0 = g * (GW // P) + 2 * pr
            c0 = g * GW + pr * LANES
            xs_p = xc[:, c0:c0 + LANES]
            seg = pair_cols(a_cs, h0) - pair_rows(a_t, h0)
            decay = jnp.where(causal, jnp.exp(jnp.where(causal, seg, 0.0)), 0.0)
            scores = cb2 * decay * pair_rows(dt_t, h0)
            xs2 = jnp.concatenate([xs_p, xs_p], axis=0)
            xs_bd = jnp.where(blk, xs2, 0.0).astype(BF16)
            y_d = _dot(scores.astype(BF16), xs_bd)
            y_o = y_off[:, pr * LANES:(pr + 1) * LANES] * pair_cols(exp_a, h0)
            y_parts.append(y_d + y_o + dsk_ref[:, c0:c0 + LANES] * xs_p)
            xw_parts.append(xs_p * pair_cols(w_s, h0))
            cd_parts.append(pair_cols(cdec, h0))
        xw = jnp.concatenate(xw_parts, axis=-1).astype(BF16)
        state_ref[g] = st_prev * jnp.concatenate(cd_parts, axis=-1) + _dot(bm_t2[:, :L], xw)
        y = jnp.concatenate(y_parts, axis=-1) * _silu(z_ref[:, g * GW:(g + 1) * GW].astype(F32))
        y = y * lax.rsqrt(jnp.mean(y * y, axis=-1, keepdims=True) + RMS_EPS)
        y_ref[:, g * GW:(g + 1) * GW] = (y * nw_ref[:, g * GW:(g + 1) * GW]).astype(BF16)


def _ssd_call(big, small, conv_w, conv_b, dtb_row, alog_row, dsk_row, nw_row, bsz, seq):
    n = big.shape[0]
    L = SSD_CHUNK
    nc = seq // L
    const = lambda b, c: (0, 0)
    return pl.pallas_call(
        _ssd_kernel,
        grid=(bsz, nc),
        in_specs=[
            pl.BlockSpec((L, SSM_INNER), lambda b, c: (b * nc + c, COL_Z // SSM_INNER)),
            pl.BlockSpec((L, SSM_CONV_DIM), lambda b, c: (b * nc + c, COL_XBC // SSM_CONV_DIM)),
            pl.BlockSpec((L, SMALL_W), lambda b, c: (b * nc + c, 0)),
            pl.BlockSpec((SSM_CONV, SSM_CONV_DIM), const),
            pl.BlockSpec((1, SSM_CONV_DIM), const),
            pl.BlockSpec((1, SMALL_W), const),
            pl.BlockSpec((1, SMALL_W), const),
            pl.BlockSpec((1, SSM_INNER), const),
            pl.BlockSpec((1, SSM_INNER), const),
        ],
        out_specs=pl.BlockSpec((L, SSM_INNER), lambda b, c: (b * nc + c, 0)),
        out_shape=jax.ShapeDtypeStruct((n, SSM_INNER), BF16),
        scratch_shapes=[
            pltpu.VMEM((8 + L, SSM_CONV_DIM), F32),
            pltpu.VMEM((SSM_GROUPS, SSM_STATE, SSM_GROUP_WIDTH), F32),
        ],
        compiler_params=_cparams(("arbitrary", "arbitrary")),
        name="ssd_scan",
    )(big, big, small, conv_w, conv_b, dtb_row, alog_row, dsk_row, nw_row)


def _mixout_kernel(x_ref, ya_ref, ys_ref, u_ref, uh_ref, gate_ref, gm_ref, shf_ref, scf_ref,
                   wao_ref, wso_ref, wp_ref, ps_ref, wout_ref, lng_ref, lnb_ref, wrh_ref, wrl_ref,
                   x1_ref, h2_ref, lt_ref, ext_ref, *, tm, tiles_per_seq):
    i = pl.program_id(0)
    halo = 16
    first = (i % tiles_per_seq) == 0
    ext_ref[0:halo, :] = jnp.where(first, 0.0, uh_ref[...].astype(F32))
    u = u_ref[...].astype(F32)
    ext_ref[halo:halo + tm, :] = u
    pos = ((i % tiles_per_seq) * tm + 1 + lax.broadcasted_iota(jnp.int32, (tm, 1), 0)).astype(F32)

    y_att = _dot(ya_ref[...], wao_ref[...])
    y_ssm = _dot(ys_ref[...], wso_ref[...])
    pool_parts = []
    for g, w in enumerate(POOL_WINDOWS):
        c0 = g * POOL_GROUP_DIM
        s = ext_ref[:, c0:c0 + POOL_GROUP_DIM]
        k = 1
        while k < w:
            s = s + pltpu.roll(s, k, 0)
            k *= 2
        pooled = s[halo:, :] / jnp.minimum(pos, float(w)) - u[:, c0:c0 + POOL_GROUP_DIM]
        pool_parts.append(_dot(pooled.astype(BF16), wp_ref[g]))
    y_pool = jnp.concatenate(pool_parts, axis=-1) * ps_ref[...]

    g_att = jax.nn.sigmoid(gate_ref[:, 0:D_MODEL].astype(F32))
    g_pool = jax.nn.sigmoid(gate_ref[:, D_MODEL:2 * D_MODEL].astype(F32))
    g_ssm = jax.nn.sigmoid(gate_ref[:, 2 * D_MODEL:3 * D_MODEL].astype(F32))
    merged = g_att * y_att + g_pool * y_pool + g_ssm * y_ssm
    mix = _dot(merged.astype(BF16), wout_ref[...])
    x1 = _layer_norm(DEEPNORM_ALPHA * x_ref[...] + gm_ref[0] * mix) * lng_ref[...] + lnb_ref[...]
    x1_ref[...] = x1
    h2 = _layer_norm(x1) * (1.0 + scf_ref[0]) + shf_ref[0]
    h2_ref[...] = h2
    h_hi, h_lo = _split_bf16(h2)
    nt = (((1,), (1,)), ((), ()))
    lt_ref[...] = (lax.dot_general(wrh_ref[...], h_hi, nt, preferred_element_type=F32)
                   + (lax.dot_general(wrh_ref[...], h_lo, nt, preferred_element_type=F32)
                      + lax.dot_general(wrl_ref[...], h_hi, nt, preferred_element_type=F32)))


def _mixout_call(x2, y_att, y_ssm, big, gm, shf, scf, wao, wso, wp, ps_row, wout, lng, lnb,
                 wr_hi, wr_lo, seq):
    n, d = x2.shape
    tm = 256
    halo = 16
    tiles_per_seq = seq // tm
    const2 = lambda i: (0, 0)
    per_seq = lambda i: (i // tiles_per_seq, 0, 0)
    return pl.pallas_call(
        functools.partial(_mixout_kernel, tm=tm, tiles_per_seq=tiles_per_seq),
        grid=(n // tm,),
        in_specs=[
            pl.BlockSpec((tm, d), lambda i: (i, 0)),
            pl.BlockSpec((tm, ATT_WIDTH), lambda i: (i, 0)),
            pl.BlockSpec((tm, SSM_INNER), lambda i: (i, 0)),
            pl.BlockSpec((tm, POOL_WIDTH), lambda i: (i, COL_POOL // POOL_WIDTH)),
            pl.BlockSpec((halo, POOL_WIDTH),
                         lambda i: (jnp.maximum(i * (tm // halo) - 1, 0), COL_POOL // POOL_WIDTH)),
            pl.BlockSpec((tm, N_BRANCHES * d), lambda i: (i, COL_GATE // (N_BRANCHES * d))),
            pl.BlockSpec((1, 1, d), per_seq),
            pl.BlockSpec((1, 1, d), per_seq),
            pl.BlockSpec((1, 1, d), per_seq),
            pl.BlockSpec((ATT_WIDTH, d), const2),
            pl.BlockSpec((SSM_INNER, d), const2),
            pl.BlockSpec((POOL_GROUPS, POOL_GROUP_DIM, POOL_GROUP_DIM), lambda i: (0, 0, 0)),
            pl.BlockSpec((1, d), const2),
            pl.BlockSpec((d, d), const2),
            pl.BlockSpec((1, d), const2),
            pl.BlockSpec((1, d), const2),
            pl.BlockSpec((N_EXPERTS, d), const2),
            pl.BlockSpec((N_EXPERTS, d), const2),
        ],
        out_specs=[
            pl.BlockSpec((tm, d), lambda i: (i, 0)),
            pl.BlockSpec((tm, d), lambda i: (i, 0)),
            pl.BlockSpec((N_EXPERTS, tm), lambda i: (0, i)),
        ],
        out_shape=[
            jax.ShapeDtypeStruct((n, d), F32),
            jax.ShapeDtypeStruct((n, d), F32),
            jax.ShapeDtypeStruct((N_EXPERTS, n), F32),
        ],
        scratch_shapes=[pltpu.VMEM((halo + tm, POOL_WIDTH), F32)],
        compiler_params=_cparams(("arbitrary",)),
        name="mixer_out",
    )(x2, y_att, y_ssm, big, big, big, gm, shf, scf, wao, wso, wp, ps_row, wout, lng, lnb,
      wr_hi, wr_lo)


def _top2(vals):
    n = len(vals)
    v1 = vals[0]
    for v in vals[1:]:
        v1 = jnp.maximum(v1, v)
    i1 = jnp.full(v1.shape, n, jnp.int32)
    for j in reversed(range(n)):
        i1 = jnp.where(vals[j] == v1, j, i1)
    v2 = jnp.full(v1.shape, -jnp.inf, F32)
    for j in range(n):
        v2 = jnp.maximum(v2, jnp.where(i1 == j, -jnp.inf, vals[j]))
    i2 = jnp.full(v1.shape, n, jnp.int32)
    for j in reversed(range(n)):
        i2 = jnp.where((vals[j] == v2) & (i1 != j), j, i2)
    return v1, i1, v2, i2


def _router_kernel(lt_ref, br_ref, e_ref, w_ref):
    lg = lt_ref[...]
    m = jnp.max(lg, axis=0, keepdims=True)
    ex = jnp.exp(lg - m)
    probs = ex / jnp.sum(ex, axis=0, keepdims=True)
    sel = probs + br_ref[...]
    tops = []
    for g in range(N_EXPERT_GROUPS):
        vals = [sel[g * EXPERTS_PER_GROUP + j:g * EXPERTS_PER_GROUP + j + 1, :]
                for j in range(EXPERTS_PER_GROUP)]
        tops.append(_top2(vals))
    best = tops[0][0] + tops[0][2]
    e1 = tops[0][1]
    e2 = tops[0][3]
    for g in range(1, N_EXPERT_GROUPS):
        score = tops[g][0] + tops[g][2]
        better = score > best
        best = jnp.where(better, score, best)
        e1 = jnp.where(better, tops[g][1] + g * EXPERTS_PER_GROUP, e1)
        e2 = jnp.where(better, tops[g][3] + g * EXPERTS_PER_GROUP, e2)
    p1 = jnp.zeros_like(best)
    p2 = jnp.zeros_like(best)
    for e in range(N_EXPERTS):
        p1 = jnp.where(e1 == e, probs[e:e + 1, :], p1)
        p2 = jnp.where(e2 == e, probs[e:e + 1, :], p2)
    tot = p1 + p2
    e_ref[...] = jnp.concatenate([e1, e2], axis=0)
    w_ref[...] = jnp.concatenate([p1 / tot, p2 / tot], axis=0)


def _router_call(logits_t, br_col):
    e, n = logits_t.shape
    t = min(2048, n)
    return pl.pallas_call(
        _router_kernel,
        grid=(n // t,),
        in_specs=[
            pl.BlockSpec((e, t), lambda i: (0, i)),
            pl.BlockSpec((e, 1), lambda i: (0, 0)),
        ],
        out_specs=[
            pl.BlockSpec((TOP_K, t), lambda i: (0, i)),
            pl.BlockSpec((TOP_K, t), lambda i: (0, i)),
        ],
        out_shape=[
            jax.ShapeDtypeStruct((TOP_K, n), jnp.int32),
            jax.ShapeDtypeStruct((TOP_K, n), F32),
        ],
        compiler_params=_cparams(("arbitrary",)),
        name="router_top2",
    )(logits_t, br_col)


def _expert_kernel(tok_ref, dst_ref, bexp_ref, nused_ref, h_hbm, w1_ref, w3_ref, w2_ref, sw_ref,
                   y_hbm, xbuf, ybuf, gsem, ssem):
    b = pl.program_id(0)
    base = b * MOE_BLOCK

    @pl.when(b < nused_ref[0])
    def _():
        def gather_copy(k):
            return pltpu.make_async_copy(h_hbm.at[pl.ds(tok_ref[base + k], 1), :],
                                         xbuf.at[pl.ds(k, 1), :], gsem)

        def scatter_copy(k):
            return pltpu.make_async_copy(ybuf.at[pl.ds(k, 1), :],
                                         y_hbm.at[pl.ds(dst_ref[base + k], 1), :], ssem)

        def for_rows(fn, valid=None):
            def body(k, c):
                if valid is None:
                    fn(k)
                else:
                    pl.when(valid(k))(lambda: fn(k))
                return c
            lax.fori_loop(0, MOE_BLOCK, body, 0)

        is_real = lambda k: dst_ref[base + k] >= 0

        for_rows(lambda k: gather_copy(k).start())
        for_rows(lambda k: gather_copy(k).wait())
        xb = xbuf[...].astype(BF16)
        hid = _silu(_dot(xb, w1_ref[0])) * _dot(xb, w3_ref[0])
        ybuf[...] = _dot(hid.astype(BF16), w2_ref[0]) * sw_ref[...]
        for_rows(lambda k: scatter_copy(k).start(), is_real)
        for_rows(lambda k: scatter_copy(k).wait(), is_real)


def _expert_call(slot_tok, slot_dst, blk_expert, n_used, h2, w1, w3, w2, slot_w):
    n, d = h2.shape
    p = slot_tok.shape[0]
    nblk = p // MOE_BLOCK
    grid_spec = pltpu.PrefetchScalarGridSpec(
        num_scalar_prefetch=4,
        grid=(nblk,),
        in_specs=[
            pl.BlockSpec(memory_space=pl.ANY),
            pl.BlockSpec((1, d, EXPERT_DFF), lambda b, tok, dst, be, nu: (be[b], 0, 0)),
            pl.BlockSpec((1, d, EXPERT_DFF), lambda b, tok, dst, be, nu: (be[b], 0, 0)),
            pl.BlockSpec((1, EXPERT_DFF, d), lambda b, tok, dst, be, nu: (be[b], 0, 0)),
            pl.BlockSpec((MOE_BLOCK, 1), lambda b, tok, dst, be, nu: (b, 0)),
        ],
        out_specs=pl.BlockSpec(memory_space=pl.ANY),
        scratch_shapes=[
            pltpu.VMEM((MOE_BLOCK, d), F32),
            pltpu.VMEM((MOE_BLOCK, d), F32),
            pltpu.SemaphoreType.DMA(()),
            pltpu.SemaphoreType.DMA(()),
        ],
    )
    return pl.pallas_call(
        _expert_kernel,
        grid_spec=grid_spec,
        out_shape=jax.ShapeDtypeStruct((n * TOP_K, d), F32),
        compiler_params=_cparams(("arbitrary",)),
        name="moe_experts",
    )(slot_tok, slot_dst, blk_expert, n_used, h2, w1, w3, w2, slot_w)


def _combine_kernel(x_ref, y_ref, gf_ref, lng_ref, lnb_ref, o_ref):
    d = x_ref.shape[1]
    ffn = y_ref[:, 0:d] + y_ref[:, d:2 * d]
    o_ref[...] = _layer_norm(DEEPNORM_ALPHA * x_ref[...] + gf_ref[0] * ffn) * lng_ref[...] + lnb_ref[...]


def _combine_call(x1, y_pairs, gf, lng, lnb, seq):
    n, d = x1.shape
    tm = 512
    tiles_per_seq = seq // tm
    return pl.pallas_call(
        _combine_kernel,
        grid=(n // tm,),
        in_specs=[
            pl.BlockSpec((tm, d), lambda i: (i, 0)),
            pl.BlockSpec((tm, TOP_K * d), lambda i: (i, 0)),
            pl.BlockSpec((1, 1, d), lambda i: (i // tiles_per_seq, 0, 0)),
            pl.BlockSpec((1, d), lambda i: (0, 0)),
            pl.BlockSpec((1, d), lambda i: (0, 0)),
        ],
        out_specs=pl.BlockSpec((tm, d), lambda i: (i, 0)),
        out_shape=jax.ShapeDtypeStruct((n, d), F32),
        compiler_params=_cparams(("arbitrary",)),
        name="moe_combine",
    )(x1, y_pairs, gf, lng, lnb)


def _slot_tables(e_idx, gate_w, n):
    nk = n * TOP_K
    e_flat = e_idx.reshape(nk)
    w_flat = gate_w.reshape(nk)
    order = jnp.argsort(e_flat).astype(jnp.int32)
    e_sorted = e_flat[order]
    counts = jnp.bincount(e_flat, length=N_EXPERTS).astype(jnp.int32)
    starts = jnp.cumsum(counts) - counts
    padded = ((counts + MOE_BLOCK - 1) // MOE_BLOCK) * MOE_BLOCK
    pends = jnp.cumsum(padded)
    pstarts = pends - padded
    dest = pstarts[e_sorted] + (jnp.arange(nk, dtype=jnp.int32) - starts[e_sorted])
    p = nk + N_EXPERTS * MOE_BLOCK
    nblk = p // MOE_BLOCK
    slot_flat = jnp.full((p,), -1, jnp.int32).at[dest].set(order)
    slot_w = jnp.zeros((p,), F32).at[dest].set(w_flat[order])
    is_pad = slot_flat < 0
    slot_tok = jnp.where(is_pad, 0, slot_flat // TOP_K)
    slot_dst = slot_flat
    blk_expert = jnp.minimum(
        jnp.searchsorted(pends, jnp.arange(nblk, dtype=jnp.int32) * MOE_BLOCK, side='right'),
        N_EXPERTS - 1).astype(jnp.int32)
    n_used = (pends[-1] // MOE_BLOCK).astype(jnp.int32).reshape(1)
    return slot_tok, slot_dst, blk_expert, n_used, slot_w.reshape(p, 1)


def kernel(x, c, w_in, b_forget, w_attn_o, w_pool, pool_scale, conv_w, conv_b, dt_bias, a_log,
           d_skip, ssm_norm_w, w_ssm_o, w_out, w_ada, b_ada, ln_mix_g, ln_mix_b, ln_ffn_g,
           ln_ffn_b, w_router, b_router, w_exp_gate, w_exp_up, w_exp_down):
    bsz, seq, d = x.shape
    n = bsz * seq
    mod = _ada_call(c, w_ada, b_ada)
    wr_hi, wr_lo = _split_bf16(w_router.T)
    br_col = b_router.reshape(N_EXPERTS, 1)
    x2 = x.reshape(n, d)
    for l in range(DEPTH):
        sh_m, sc_m, g_m, sh_f, sc_f, g_f = (mod[l, :, i * d:(i + 1) * d].reshape(bsz, 1, d)
                                            for i in range(6))
        wl = w_in[l]
        o_f = 3 * ATT_WIDTH
        o_pool = o_f + ATT_HEADS
        o_z = o_pool + POOL_WIDTH
        o_xbc = o_z + SSM_INNER
        o_dt = o_xbc + SSM_CONV_DIM
        o_gate = o_dt + SSM_HEADS
        w_big = jnp.concatenate([wl[:, :o_f], wl[:, o_pool:o_dt], wl[:, o_gate:]], axis=1).astype(BF16)
        w_small = jnp.concatenate(
            [wl[:, o_dt:o_gate], wl[:, o_f:o_pool],
             jnp.zeros((d, SMALL_W - SSM_HEADS - ATT_HEADS), F32)], axis=1)
        ws_hi, ws_lo = _split_bf16(w_small)
        big, small = _inproj_call(x2, sh_m, sc_m, w_big, ws_hi, ws_lo, seq)

        bf_row = jnp.zeros((1, SMALL_W), F32).at[0, F_COL0:F_COL0 + ATT_HEADS].set(b_forget[l])
        fcol = _fcum_call(small, bf_row, bsz, seq)
        frow = jnp.transpose(fcol.reshape(bsz, seq, SMALL_W)[:, :, F_COL0:F_COL0 + ATT_HEADS],
                             (0, 2, 1)).reshape(bsz * ATT_HEADS, 1, seq)
        y_att = _attn_call(big.reshape(bsz, seq, BIG_W), fcol, frow).reshape(n, ATT_WIDTH)

        pad_heads = jnp.zeros((SMALL_W - SSM_HEADS,), F32)
        dtb_row = jnp.concatenate([dt_bias[l], pad_heads]).reshape(1, SMALL_W)
        alog_row = jnp.concatenate([a_log[l], pad_heads]).reshape(1, SMALL_W)
        dsk_row = jnp.repeat(d_skip[l], SSM_HEAD_DIM).reshape(1, SSM_INNER)
        y_ssm = _ssd_call(big, small, conv_w[l], conv_b[l].reshape(1, SSM_CONV_DIM), dtb_row,
                          alog_row, dsk_row, ssm_norm_w[l].reshape(1, SSM_INNER), bsz, seq)

        x1, h2, logits_t = _mixout_call(
            x2, y_att, y_ssm, big, g_m, sh_f, sc_f, w_attn_o[l].astype(BF16),
            w_ssm_o[l].astype(BF16), w_pool[l].astype(BF16), pool_scale[l].reshape(1, d),
            w_out[l].astype(BF16), ln_mix_g[l].reshape(1, d), ln_mix_b[l].reshape(1, d),
            wr_hi, wr_lo, seq)

        e_idx_t, gate_t = _router_call(logits_t, br_col)
        slot_tok, slot_dst, blk_expert, n_used, slot_w = _slot_tables(e_idx_t.T, gate_t.T, n)
        y_slots = _expert_call(slot_tok, slot_dst, blk_expert, n_used, h2,
                               w_exp_gate[l].astype(BF16), w_exp_up[l].astype(BF16),
                               w_exp_down[l].astype(BF16), slot_w)
        y_pairs = y_slots.reshape(y_slots.shape[0] // TOP_K, TOP_K * d)
        x2 = _combine_call(x1, y_pairs, g_f, ln_ffn_g[l].reshape(1, d), ln_ffn_b[l].reshape(1, d), seq)
    return x2.reshape(bsz, seq, d)
```

```python
import functools

import jax
import jax.numpy as jnp
from jax import lax
from jax.experimental import pallas as pl
from jax.experimental.pallas import tpu as pltpu

F32 = jnp.float32
BF16 = jnp.bfloat16

D_MODEL = 1024
DEPTH = 2
ATT_HEADS = 16
ATT_HEAD_DIM = 64
ATT_WIDTH = ATT_HEADS * ATT_HEAD_DIM
POOL_WINDOWS = (2, 4, 8, 16)
POOL_GROUPS = len(POOL_WINDOWS)
POOL_WIDTH = D_MODEL
POOL_GROUP_DIM = POOL_WIDTH // POOL_GROUPS
SSM_INNER = 2 * D_MODEL
SSM_HEAD_DIM = 64
SSM_HEADS = SSM_INNER // SSM_HEAD_DIM
SSM_GROUPS = 4
SSM_STATE = 128
SSM_CONV = 4
SSM_CONV_DIM = SSM_INNER + 2 * SSM_GROUPS * SSM_STATE
SSM_GROUP_WIDTH = SSM_INNER // SSM_GROUPS
N_BRANCHES = 3
N_EXPERTS = 16
N_EXPERT_GROUPS = 4
EXPERTS_PER_GROUP = N_EXPERTS // N_EXPERT_GROUPS
TOP_K = 2
EXPERT_DFF = 512
MOE_BLOCK = 256
DEEPNORM_ALPHA = (2 * DEPTH) ** 0.25
LN_EPS = 1e-5
RMS_EPS = 1e-6

LANES = 128
SSD_CHUNK = 64
ROW_DMA_UNROLL = 8
SMALL_W = LANES
F_COL0 = SSM_HEADS
COL_Q, COL_K, COL_V = 0, ATT_WIDTH, 2 * ATT_WIDTH
COL_POOL = 3 * ATT_WIDTH
COL_Z = COL_POOL + POOL_WIDTH
COL_XBC = COL_Z + SSM_INNER
COL_GATE = COL_XBC + SSM_CONV_DIM
BIG_W = COL_GATE + N_BRANCHES * D_MODEL
VMEM_LIMIT = 48 * 1024 * 1024


def _cparams(sem):
    return pltpu.CompilerParams(dimension_semantics=sem, vmem_limit_bytes=VMEM_LIMIT)


def _silu(v):
    return v * jax.nn.sigmoid(v)


def _softplus(v):
    return jnp.maximum(v, 0.0) + jnp.log1p(jnp.exp(-jnp.abs(v)))


def _log_sigmoid(v):
    return jnp.minimum(v, 0.0) - jnp.log1p(jnp.exp(-jnp.abs(v)))


def _layer_norm(v):
    mu = jnp.mean(v, axis=-1, keepdims=True)
    vc = v - mu
    return vc * lax.rsqrt(jnp.mean(vc * vc, axis=-1, keepdims=True) + LN_EPS)


def _split_bf16(v):
    hi = v.astype(BF16)
    lo = (v - hi.astype(F32)).astype(BF16)
    return hi, lo


def _dot(a, b):
    return jnp.dot(a, b, preferred_element_type=F32)


def _dot3(a_hi, a_lo, b_hi, b_lo):
    return _dot(a_hi, b_hi) + (_dot(a_lo, b_hi) + _dot(a_hi, b_lo))


def _cumsum_rows(v, rows):
    t = v.shape[0]
    k = 1
    while k < t:
        v = v + jnp.where(rows >= k, pltpu.roll(v, k, 0), 0.0)
        k *= 2
    return v


def _ada_kernel(c_ref, w_ref, b_ref, o_ref):
    cond = _silu(c_ref[...])
    c_hi, c_lo = _split_bf16(cond)
    w_hi, w_lo = _split_bf16(w_ref[0])
    o_ref[0] = _dot3(c_hi, c_lo, w_hi, w_lo) + b_ref[0]


def _ada_call(c, w_ada, b_ada):
    depth, d, width = w_ada.shape
    bsz = c.shape[0]
    rows = 8
    c_pad = jnp.zeros((rows, d), F32).at[:bsz].set(c)
    tn = 1024
    out = pl.pallas_call(
        _ada_kernel,
        grid=(depth, width // tn),
        in_specs=[
            pl.BlockSpec((rows, d), lambda l, j: (0, 0)),
            pl.BlockSpec((1, d, tn), lambda l, j: (l, 0, j)),
            pl.BlockSpec((1, 1, tn), lambda l, j: (l, 0, j)),
        ],
        out_specs=pl.BlockSpec((1, rows, tn), lambda l, j: (l, 0, j)),
        out_shape=jax.ShapeDtypeStruct((depth, rows, width), F32),
        compiler_params=_cparams(("arbitrary", "arbitrary")),
        name="ada_mod",
    )(c_pad, w_ada, b_ada.reshape(depth, 1, width))
    return out[:, :bsz]


def _inproj_kernel(x_ref, sh_ref, sc_ref, w_ref, wsh_ref, wsl_ref, big_ref, small_ref, h_ref):
    @pl.when(pl.program_id(1) == 0)
    def _():
        h = _layer_norm(x_ref[...]) * (1.0 + sc_ref[0]) + sh_ref[0]
        h_hi, h_lo = _split_bf16(h)
        h_ref[...] = h_hi
        small_ref[...] = _dot3(h_hi, h_lo, wsh_ref[...], wsl_ref[...])

    big_ref[...] = _dot(h_ref[...], w_ref[...]).astype(BF16)


def _inproj_call(x2, shift, scale, w_big, ws_hi, ws_lo, seq):
    n, d = x2.shape
    tm, tn = min(1024, seq), 1024
    tiles_per_seq = seq // tm
    return pl.pallas_call(
        _inproj_kernel,
        grid=(n // tm, BIG_W // tn),
        in_specs=[
            pl.BlockSpec((tm, d), lambda i, j: (i, 0)),
            pl.BlockSpec((1, 1, d), lambda i, j: (i // tiles_per_seq, 0, 0)),
            pl.BlockSpec((1, 1, d), lambda i, j: (i // tiles_per_seq, 0, 0)),
            pl.BlockSpec((d, tn), lambda i, j: (0, j)),
            pl.BlockSpec((d, SMALL_W), lambda i, j: (0, 0)),
            pl.BlockSpec((d, SMALL_W), lambda i, j: (0, 0)),
        ],
        out_specs=[
            pl.BlockSpec((tm, tn), lambda i, j: (i, j)),
            pl.BlockSpec((tm, SMALL_W), lambda i, j: (i, 0)),
        ],
        out_shape=[
            jax.ShapeDtypeStruct((n, BIG_W), BF16),
            jax.ShapeDtypeStruct((n, SMALL_W), F32),
        ],
        scratch_shapes=[pltpu.VMEM((tm, d), BF16)],
        compiler_params=_cparams(("arbitrary", "arbitrary")),
        name="inproj",
    )(x2, shift, scale, w_big, ws_hi, ws_lo)


def _fcum_kernel(s_ref, bf_ref, o_ref, carry_ref):
    @pl.when(pl.program_id(1) == 0)
    def _():
        carry_ref[...] = jnp.zeros_like(carry_ref)

    lf = _log_sigmoid(s_ref[...] + bf_ref[...])
    rows = lax.broadcasted_iota(jnp.int32, lf.shape, 0)
    out = _cumsum_rows(lf, rows) + carry_ref[0:1, :]
    o_ref[...] = out
    carry_ref[0:1, :] = out[lf.shape[0] - 1:, :]


def _fcum_call(small, bf_row, bsz, seq):
    t = 512
    per = seq // t
    return pl.pallas_call(
        _fcum_kernel,
        grid=(bsz, per),
        in_specs=[
            pl.BlockSpec((t, SMALL_W), lambda b, i: (b * per + i, 0)),
            pl.BlockSpec((1, SMALL_W), lambda b, i: (0, 0)),
        ],
        out_specs=pl.BlockSpec((t, SMALL_W), lambda b, i: (b * per + i, 0)),
        out_shape=jax.ShapeDtypeStruct(small.shape, F32),
        scratch_shapes=[pltpu.VMEM((8, SMALL_W), F32)],
        compiler_params=_cparams(("arbitrary", "arbitrary")),
        name="forget_cumsum",
    )(small, bf_row)


def _split3_bf16(v):
    hi = v.astype(BF16)
    r = v - hi.astype(F32)
    mid = r.astype(BF16)
    lo = (r - mid.astype(F32)).astype(BF16)
    return hi, mid, lo


def _attn_kernel(q_ref, k_ref, v_ref, f_ref, o_ref, kx_ref, vx_ref, *, tq):
    hp = pl.program_id(1)
    qi = pl.program_id(2)
    dh = ATT_HEAD_DIM
    seq = k_ref.shape[0]

    def head_lanes(rows, hh):
        lane = lax.broadcasted_iota(jnp.int32, (rows, LANES), 1)
        in_head = (lane >= hh * dh) & (lane < (hh + 1) * dh)
        return lane - (1 - hh) * dh, in_head

    def f_column(f_tile, hh):
        lane = lax.broadcasted_iota(jnp.int32, f_tile.shape, 1)
        return jnp.sum(jnp.where(lane == F_COL0 + 2 * hp + hh, f_tile, 0.0), axis=-1, keepdims=True)

    def f_terms(f_tile, hh):
        return [t.astype(F32) for t in _split3_bf16(f_column(f_tile, hh))]

    @pl.when(qi == 0)
    def _():
        f_all = f_ref[...]
        for hh in range(2):
            ext, in_head = head_lanes(seq, hh)
            hi, mid, lo = f_terms(f_all, hh)
            k_sp = jnp.where((ext >= 0) & (ext < 3), 1.0,
                             jnp.where(ext == 3, -hi, jnp.where(ext == 4, -mid,
                                                                jnp.where(ext == 5, -lo, 0.0))))
            kx_ref[hh] = jnp.where(in_head, k_ref[...], k_sp.astype(BF16))
            vx_ref[hh] = jnp.where(in_head, v_ref[...], jnp.where(ext == 0, 1.0, 0.0).astype(BF16))

    f_q = f_ref[pl.ds(pl.multiple_of(qi * tq, tq), tq), :]
    r_iota = lax.broadcasted_iota(jnp.int32, (tq, tq), 0)
    c_iota = lax.broadcasted_iota(jnp.int32, (tq, tq), 1)
    nt = (((1,), (1,)), ((), ()))
    qx = []
    for hh in range(2):
        ext, in_head = head_lanes(tq, hh)
        hi, mid, lo = f_terms(f_q, hh)
        q_sp = jnp.where(ext == 0, hi, jnp.where(ext == 1, mid, jnp.where(ext == 2, lo,
                         jnp.where((ext >= 3) & (ext < 6), 1.0, 0.0))))
        qx.append(jnp.where(in_head, q_ref[...] * 0.125, q_sp.astype(BF16)))

    def block(hh, j, m, acc, diag):
        off = pl.multiple_of(j * tq, tq)
        s = lax.dot_general(qx[hh], kx_ref[hh, pl.ds(off, tq), :], nt, preferred_element_type=F32)
        if diag:
            s = jnp.where(r_iota >= c_iota, s, -jnp.inf)
        s_max = jnp.max(s, axis=-1, keepdims=True)
        m_new = s_max if m is None else jnp.maximum(m, s_max)
        p = jnp.exp((s - m_new).astype(BF16))
        pv = _dot(p, vx_ref[hh, pl.ds(off, tq), :])
        if m is None:
            return m_new, pv
        return m_new, jnp.exp(m - m_new) * acc + pv

    m0, a0 = block(0, qi, None, None, True)
    m1, a1 = block(1, qi, None, None, True)

    def body(j, carry):
        m0, a0, m1, a1 = carry
        m0, a0 = block(0, j, m0, a0, False)
        m1, a1 = block(1, j, m1, a1, False)
        return m0, a0, m1, a1

    _, a0, _, a1 = lax.fori_loop(0, qi, body, (m0, a0, m1, a1))
    out = None
    for hh, acc in ((0, a0), (1, a1)):
        ext, in_head = head_lanes(tq, hh)
        denom = jnp.sum(jnp.where(ext == 0, acc, 0.0), axis=-1, keepdims=True)
        o = acc / denom
        out = o if out is None else jnp.where(in_head, o, out)
    o_ref[...] = out.astype(BF16)


def _attn_call(big3, fcol):
    bsz, seq, _ = big3.shape
    tq = min(512, seq)
    nq = seq // tq
    pairs = ATT_HEADS // 2
    kb, vb = COL_K // LANES, COL_V // LANES
    return pl.pallas_call(
        functools.partial(_attn_kernel, tq=tq),
        grid=(bsz, pairs, nq),
        in_specs=[
            pl.BlockSpec((None, tq, LANES), lambda b, h, i: (b, i, h)),
            pl.BlockSpec((None, seq, LANES), lambda b, h, i: (b, 0, kb + h)),
            pl.BlockSpec((None, seq, LANES), lambda b, h, i: (b, 0, vb + h)),
            pl.BlockSpec((seq, SMALL_W), lambda b, h, i: (b, 0)),
        ],
        out_specs=pl.BlockSpec((None, tq, LANES), lambda b, h, i: (b, i, h)),
        out_shape=jax.ShapeDtypeStruct((bsz, seq, ATT_WIDTH), BF16),
        scratch_shapes=[pltpu.VMEM((2, seq, LANES), BF16), pltpu.VMEM((2, seq, LANES), BF16)],
        compiler_params=_cparams(("arbitrary", "arbitrary", "arbitrary")),
        name="fox_attention",
    )(big3, big3, big3, fcol)


def _ssd_kernel(z_ref, xbc_ref, dt_ref, cw_ref, cb_ref, dtb_ref, alog_ref, dsk_ref, nw_ref,
                y_ref, ext_ref, state_ref):
    L = SSD_CHUNK
    P = SSM_HEAD_DIM
    GW = SSM_GROUP_WIDTH
    NS = SSM_STATE
    halo = 8

    @pl.when(pl.program_id(1) == 0)
    def _():
        ext_ref[0:halo, :] = jnp.zeros((halo, SSM_CONV_DIM), F32)
        state_ref[...] = jnp.zeros_like(state_ref)

    ext_ref[halo:halo + L, :] = xbc_ref[...].astype(F32)
    conv = cb_ref[...]
    for j in range(SSM_CONV):
        lo = halo - (SSM_CONV - 1) + j
        conv = conv + cw_ref[j:j + 1, :] * ext_ref[lo:lo + L, :]
    ext_ref[0:halo, :] = ext_ref[L:L + halo, :]
    xc = _silu(conv)

    rows = lax.broadcasted_iota(jnp.int32, (L, LANES), 0)
    lane = lax.broadcasted_iota(jnp.int32, (L, LANES), 1)
    lo_half = lane < P
    dt = _softplus(dt_ref[...] + dtb_ref[...])
    a_cs = _cumsum_rows(dt * (-jnp.exp(alog_ref[...])), rows)
    a_last = a_cs[L - 1:L, :]
    exp_a = jnp.exp(a_cs)
    w_s = jnp.exp(a_last - a_cs) * dt
    cdec = jnp.exp(a_last)
    a_t = jnp.concatenate([a_cs, a_cs], axis=0).T
    dt_t = jnp.concatenate([dt, dt], axis=0).T
    causal = rows >= jnp.where(lo_half, lane, lane - P)
    blk = (lax.broadcasted_iota(jnp.int32, (2 * L, LANES), 0) < L) == \
          (lax.broadcasted_iota(jnp.int32, (2 * L, LANES), 1) < P)

    def pair_cols(mat, h0):
        r = mat.shape[0]
        c0 = jnp.broadcast_to(mat[:, h0:h0 + 1], (r, LANES))
        c1 = jnp.broadcast_to(mat[:, h0 + 1:h0 + 2], (r, LANES))
        return jnp.where(lo_half[:r], c0, c1)

    def pair_rows(mat_t, h0):
        return jnp.where(lo_half[:1], mat_t[h0:h0 + 1, :], mat_t[h0 + 1:h0 + 2, :])

    for g in range(SSM_GROUPS):
        bm = xc[:, SSM_INNER + g * NS:SSM_INNER + (g + 1) * NS]
        cm = xc[:, SSM_INNER + SSM_GROUPS * NS + g * NS:SSM_INNER + SSM_GROUPS * NS + (g + 1) * NS]
        cm_b = cm.astype(BF16)
        bm_t2 = jnp.concatenate([bm, bm], axis=0).T.astype(BF16)
        cb2 = _dot(cm_b, bm_t2)
        st_prev = state_ref[g]
        y_off = _dot(cm_b, st_prev.astype(BF16))
        y_parts, xw_parts, cd_parts = [], [], []
        for pr in range(GW // LANES):
            h0 = g * (GW // P) + 2 * pr
            c0 = g * GW + pr * LANES
            xs_p = xc[:, c0:c0 + LANES]
            seg = pair_cols(a_cs, h0) - pair_rows(a_t, h0)
            decay = jnp.where(causal, jnp.exp(jnp.where(causal, seg, 0.0)), 0.0)
            scores = cb2 * decay * pair_rows(dt_t, h0)
            xs2 = jnp.concatenate([xs_p, xs_p], axis=0)
            xs_bd = jnp.where(blk, xs2, 0.0).astype(BF16)
            y_d = _dot(scores.astype(BF16), xs_bd)
            y_o = y_off[:, pr * LANES:(pr + 1) * LANES] * pair_cols(exp_a, h0)
            y_parts.append(y_d + y_o + dsk_ref[:, c0:c0 + LANES] * xs_p)
            xw_parts.append(xs_p * pair_cols(w_s, h0))
            cd_parts.append(pair_cols(cdec, h0))
        xw = jnp.concatenate(xw_parts, axis=-1).astype(BF16)
        state_ref[g] = st_prev * jnp.concatenate(cd_parts, axis=-1) + _dot(bm_t2[:, :L], xw)
        y = jnp.concatenate(y_parts, axis=-1) * _silu(z_ref[:, g * GW:(g + 1) * GW].astype(F32))
        y = y * lax.rsqrt(jnp.mean(y * y, axis=-1, keepdims=True) + RMS_EPS)
        y_ref[:, g * GW:(g + 1) * GW] = (y * nw_ref[:, g * GW:(g + 1) * GW]).astype(BF16)


def _ssd_call(big, small, conv_w, conv_b, dtb_row, alog_row, dsk_row, nw_row, bsz, seq):
    n = big.shape[0]
    L = SSD_CHUNK
    nc = seq // L
    const = lambda b, c: (0, 0)
    return pl.pallas_call(
        _ssd_kernel,
        grid=(bsz, nc),
        in_specs=[
            pl.BlockSpec((L, SSM_INNER), lambda b, c: (b * nc + c, COL_Z // SSM_INNER)),
            pl.BlockSpec((L, SSM_CONV_DIM), lambda b, c: (b * nc + c, COL_XBC // SSM_CONV_DIM)),
            pl.BlockSpec((L, SMALL_W), lambda b, c: (b * nc + c, 0)),
            pl.BlockSpec((SSM_CONV, SSM_CONV_DIM), const),
            pl.BlockSpec((1, SSM_CONV_DIM), const),
            pl.BlockSpec((1, SMALL_W), const),
            pl.BlockSpec((1, SMALL_W), const),
            pl.BlockSpec((1, SSM_INNER), const),
            pl.BlockSpec((1, SSM_INNER), const),
        ],
        out_specs=pl.BlockSpec((L, SSM_INNER), lambda b, c: (b * nc + c, 0)),
        out_shape=jax.ShapeDtypeStruct((n, SSM_INNER), BF16),
        scratch_shapes=[
            pltpu.VMEM((8 + L, SSM_CONV_DIM), F32),
            pltpu.VMEM((SSM_GROUPS, SSM_STATE, SSM_GROUP_WIDTH), F32),
        ],
        compiler_params=_cparams(("arbitrary", "arbitrary")),
        name="ssd_scan",
    )(big, big, small, conv_w, conv_b, dtb_row, alog_row, dsk_row, nw_row)


def _mixout_kernel(x_ref, ya_ref, ys_ref, u_ref, uh_ref, gate_ref, gm_ref, shf_ref, scf_ref,
                   wao_ref, wso_ref, wp_ref, ps_ref, wout_ref, lng_ref, lnb_ref, wrh_ref, wrl_ref,
                   x1_ref, h2_ref, lt_ref, ext_ref, *, tm, tiles_per_seq):
    i = pl.program_id(0)
    halo = 16
    first = (i % tiles_per_seq) == 0
    ext_ref[0:halo, :] = jnp.where(first, 0.0, uh_ref[...].astype(F32))
    u = u_ref[...].astype(F32)
    ext_ref[halo:halo + tm, :] = u
    pos = ((i % tiles_per_seq) * tm + 1 + lax.broadcasted_iota(jnp.int32, (tm, 1), 0)).astype(F32)

    y_att = _dot(ya_ref[...], wao_ref[...])
    y_ssm = _dot(ys_ref[...], wso_ref[...])
    pool_parts = []
    for g, w in enumerate(POOL_WINDOWS):
        c0 = g * POOL_GROUP_DIM
        s = ext_ref[:, c0:c0 + POOL_GROUP_DIM]
        k = 1
        while k < w:
            s = s + pltpu.roll(s, k, 0)
            k *= 2
        pooled = s[halo:, :] / jnp.minimum(pos, float(w)) - u[:, c0:c0 + POOL_GROUP_DIM]
        pool_parts.append(_dot(pooled.astype(BF16), wp_ref[g]))
    y_pool = jnp.concatenate(pool_parts, axis=-1) * ps_ref[...]

    g_att = jax.nn.sigmoid(gate_ref[:, 0:D_MODEL].astype(F32))
    g_pool = jax.nn.sigmoid(gate_ref[:, D_MODEL:2 * D_MODEL].astype(F32))
    g_ssm = jax.nn.sigmoid(gate_ref[:, 2 * D_MODEL:3 * D_MODEL].astype(F32))
    merged = g_att * y_att + g_pool * y_pool + g_ssm * y_ssm
    mix = _dot(merged.astype(BF16), wout_ref[...])
    x1 = _layer_norm(DEEPNORM_ALPHA * x_ref[...] + gm_ref[0] * mix) * lng_ref[...] + lnb_ref[...]
    x1_ref[...] = x1
    h2 = _layer_norm(x1) * (1.0 + scf_ref[0]) + shf_ref[0]
    h2_ref[...] = h2
    h_hi, h_lo = _split_bf16(h2)
    nt = (((1,), (1,)), ((), ()))
    lt_ref[...] = (lax.dot_general(wrh_ref[...], h_hi, nt, preferred_element_type=F32)
                   + (lax.dot_general(wrh_ref[...], h_lo, nt, preferred_element_type=F32)
                      + lax.dot_general(wrl_ref[...], h_hi, nt, preferred_element_type=F32)))


def _mixout_call(x2, y_att, y_ssm, big, gm, shf, scf, wao, wso, wp, ps_row, wout, lng, lnb,
                 wr_hi, wr_lo, seq):
    n, d = x2.shape
    tm = 256
    halo = 16
    tiles_per_seq = seq // tm
    const2 = lambda i: (0, 0)
    per_seq = lambda i: (i // tiles_per_seq, 0, 0)
    return pl.pallas_call(
        functools.partial(_mixout_kernel, tm=tm, tiles_per_seq=tiles_per_seq),
        grid=(n // tm,),
        in_specs=[
            pl.BlockSpec((tm, d), lambda i: (i, 0)),
            pl.BlockSpec((tm, ATT_WIDTH), lambda i: (i, 0)),
            pl.BlockSpec((tm, SSM_INNER), lambda i: (i, 0)),
            pl.BlockSpec((tm, POOL_WIDTH), lambda i: (i, COL_POOL // POOL_WIDTH)),
            pl.BlockSpec((halo, POOL_WIDTH),
                         lambda i: (jnp.maximum(i * (tm // halo) - 1, 0), COL_POOL // POOL_WIDTH)),
            pl.BlockSpec((tm, N_BRANCHES * d), lambda i: (i, COL_GATE // (N_BRANCHES * d))),
            pl.BlockSpec((1, 1, d), per_seq),
            pl.BlockSpec((1, 1, d), per_seq),
            pl.BlockSpec((1, 1, d), per_seq),
            pl.BlockSpec((ATT_WIDTH, d), const2),
            pl.BlockSpec((SSM_INNER, d), const2),
            pl.BlockSpec((POOL_GROUPS, POOL_GROUP_DIM, POOL_GROUP_DIM), lambda i: (0, 0, 0)),
            pl.BlockSpec((1, d), const2),
            pl.BlockSpec((d, d), const2),
            pl.BlockSpec((1, d), const2),
            pl.BlockSpec((1, d), const2),
            pl.BlockSpec((N_EXPERTS, d), const2),
            pl.BlockSpec((N_EXPERTS, d), const2),
        ],
        out_specs=[
            pl.BlockSpec((tm, d), lambda i: (i, 0)),
            pl.BlockSpec((tm, d), lambda i: (i, 0)),
            pl.BlockSpec((N_EXPERTS, tm), lambda i: (0, i)),
        ],
        out_shape=[
            jax.ShapeDtypeStruct((n, d), F32),
            jax.ShapeDtypeStruct((n, d), F32),
            jax.ShapeDtypeStruct((N_EXPERTS, n), F32),
        ],
        scratch_shapes=[pltpu.VMEM((halo + tm, POOL_WIDTH), F32)],
        compiler_params=_cparams(("arbitrary",)),
        name="mixer_out",
    )(x2, y_att, y_ssm, big, big, big, gm, shf, scf, wao, wso, wp, ps_row, wout, lng, lnb,
      wr_hi, wr_lo)


def _top2(vals):
    n = len(vals)
    v1 = vals[0]
    for v in vals[1:]:
        v1 = jnp.maximum(v1, v)
    i1 = jnp.full(v1.shape, n, jnp.int32)
    for j in reversed(range(n)):
        i1 = jnp.where(vals[j] == v1, j, i1)
    v2 = jnp.full(v1.shape, -jnp.inf, F32)
    for j in range(n):
        v2 = jnp.maximum(v2, jnp.where(i1 == j, -jnp.inf, vals[j]))
    i2 = jnp.full(v1.shape, n, jnp.int32)
    for j in reversed(range(n)):
        i2 = jnp.where((vals[j] == v2) & (i1 != j), j, i2)
    return v1, i1, v2, i2


def _router_kernel(lt_ref, br_ref, e_ref, w_ref):
    lg = lt_ref[...]
    m = jnp.max(lg, axis=0, keepdims=True)
    ex = jnp.exp(lg - m)
    probs = ex / jnp.sum(ex, axis=0, keepdims=True)
    sel = probs + br_ref[...]
    tops = []
    for g in range(N_EXPERT_GROUPS):
        vals = [sel[g * EXPERTS_PER_GROUP + j:g * EXPERTS_PER_GROUP + j + 1, :]
                for j in range(EXPERTS_PER_GROUP)]
        tops.append(_top2(vals))
    best = tops[0][0] + tops[0][2]
    e1 = tops[0][1]
    e2 = tops[0][3]
    for g in range(1, N_EXPERT_GROUPS):
        score = tops[g][0] + tops[g][2]
        better = score > best
        best = jnp.where(better, score, best)
        e1 = jnp.where(better, tops[g][1] + g * EXPERTS_PER_GROUP, e1)
        e2 = jnp.where(better, tops[g][3] + g * EXPERTS_PER_GROUP, e2)
    p1 = jnp.zeros_like(best)
    p2 = jnp.zeros_like(best)
    for e in range(N_EXPERTS):
        p1 = jnp.where(e1 == e, probs[e:e + 1, :], p1)
        p2 = jnp.where(e2 == e, probs[e:e + 1, :], p2)
    tot = p1 + p2
    e_ref[...] = jnp.concatenate([e1, e2], axis=0)
    w_ref[...] = jnp.concatenate([p1 / tot, p2 / tot], axis=0)


def _router_call(logits_t, br_col):
    e, n = logits_t.shape
    t = min(2048, n)
    return pl.pallas_call(
        _router_kernel,
        grid=(n // t,),
        in_specs=[
            pl.BlockSpec((e, t), lambda i: (0, i)),
            pl.BlockSpec((e, 1), lambda i: (0, 0)),
        ],
        out_specs=[
            pl.BlockSpec((TOP_K, t), lambda i: (0, i)),
            pl.BlockSpec((TOP_K, t), lambda i: (0, i)),
        ],
        out_shape=[
            jax.ShapeDtypeStruct((TOP_K, n), jnp.int32),
            jax.ShapeDtypeStruct((TOP_K, n), F32),
        ],
        compiler_params=_cparams(("arbitrary",)),
        name="router_top2",
    )(logits_t, br_col)


def _expert_kernel(tok_ref, dst_ref, bexp_ref, nused_ref, h_hbm, w1_ref, w3_ref, w2_ref, sw_ref,
                   y_hbm, xbuf, ybuf, gsem, ssem):
    b = pl.program_id(0)
    nblk = pl.num_programs(0)
    n_used = nused_ref[0]
    slot = b % 2
    d = xbuf.shape[2]

    def start_gather(blk, sl):
        base = blk * MOE_BLOCK

        def body(k, c):
            pltpu.make_async_copy(h_hbm.at[pl.ds(tok_ref[base + k], 1), :],
                                  xbuf.at[sl, pl.ds(k, 1), :], gsem.at[sl]).start()
            return c

        lax.fori_loop(0, MOE_BLOCK, body, 0, unroll=ROW_DMA_UNROLL)

    def start_scatter(blk, sl):
        base = blk * MOE_BLOCK

        def body(k, c):
            flat = dst_ref[base + k]
            col = pl.multiple_of((flat & 1) * d, d)
            pltpu.make_async_copy(ybuf.at[sl, pl.ds(k, 1), :],
                                  y_hbm.at[pl.ds(flat >> 1, 1), pl.ds(col, d)], ssem.at[sl]).start()
            return c

        lax.fori_loop(0, MOE_BLOCK, body, 0, unroll=ROW_DMA_UNROLL)

    def wait_gather(sl):
        pltpu.make_async_copy(h_hbm.at[pl.ds(0, MOE_BLOCK), :], xbuf.at[sl], gsem.at[sl]).wait()

    def wait_scatter(sl):
        pltpu.make_async_copy(ybuf.at[sl], y_hbm.at[pl.ds(0, MOE_BLOCK), pl.ds(0, d)], ssem.at[sl]).wait()

    @pl.when(b == 0)
    def _():
        ybuf[...] = jnp.zeros_like(ybuf)
        start_gather(0, 0)

    @pl.when(b + 1 < n_used)
    def _():
        start_gather(b + 1, 1 - slot)

    @pl.when(b >= 2)
    def _():
        wait_scatter(slot)

    @pl.when(b < n_used)
    def _():
        wait_gather(slot)
        xb = xbuf[slot].astype(BF16)
        hid = _silu(_dot(xb, w1_ref[0])) * _dot(xb, w3_ref[0])
        ybuf[slot] = _dot(hid.astype(BF16), w2_ref[0]) * sw_ref[...]

    start_scatter(b, slot)

    @pl.when(b == nblk - 1)
    def _():
        wait_scatter(1 - slot)
        wait_scatter(slot)


def _expert_call(slot_tok, slot_dst, blk_expert, n_used, h2, w1, w3, w2, slot_w):
    n, d = h2.shape
    p = slot_tok.shape[0]
    nblk = p // MOE_BLOCK
    grid_spec = pltpu.PrefetchScalarGridSpec(
        num_scalar_prefetch=4,
        grid=(nblk,),
        in_specs=[
            pl.BlockSpec(memory_space=pl.ANY),
            pl.BlockSpec((1, d, EXPERT_DFF), lambda b, tok, dst, be, nu: (be[b], 0, 0)),
            pl.BlockSpec((1, d, EXPERT_DFF), lambda b, tok, dst, be, nu: (be[b], 0, 0)),
            pl.BlockSpec((1, EXPERT_DFF, d), lambda b, tok, dst, be, nu: (be[b], 0, 0)),
            pl.BlockSpec((MOE_BLOCK, 1), lambda b, tok, dst, be, nu: (b, 0)),
        ],
        out_specs=pl.BlockSpec(memory_space=pl.ANY),
        scratch_shapes=[
            pltpu.VMEM((2, MOE_BLOCK, d), F32),
            pltpu.VMEM((2, MOE_BLOCK, d), F32),
            pltpu.SemaphoreType.DMA((2,)),
            pltpu.SemaphoreType.DMA((2,)),
        ],
    )
    return pl.pallas_call(
        _expert_kernel,
        grid_spec=grid_spec,
        out_shape=jax.ShapeDtypeStruct((p // TOP_K, TOP_K * d), F32),
        compiler_params=_cparams(("arbitrary",)),
        name="moe_experts",
    )(slot_tok, slot_dst, blk_expert, n_used, h2, w1, w3, w2, slot_w)


def _combine_kernel(x_ref, y_ref, gf_ref, lng_ref, lnb_ref, o_ref):
    d = x_ref.shape[1]
    ffn = y_ref[:, 0:d] + y_ref[:, d:2 * d]
    o_ref[...] = _layer_norm(DEEPNORM_ALPHA * x_ref[...] + gf_ref[0] * ffn) * lng_ref[...] + lnb_ref[...]


def _combine_call(x1, y_pairs, gf, lng, lnb, seq):
    n, d = x1.shape
    tm = 512
    tiles_per_seq = seq // tm
    return pl.pallas_call(
        _combine_kernel,
        grid=(n // tm,),
        in_specs=[
            pl.BlockSpec((tm, d), lambda i: (i, 0)),
            pl.BlockSpec((tm, TOP_K * d), lambda i: (i, 0)),
            pl.BlockSpec((1, 1, d), lambda i: (i // tiles_per_seq, 0, 0)),
            pl.BlockSpec((1, d), lambda i: (0, 0)),
            pl.BlockSpec((1, d), lambda i: (0, 0)),
        ],
        out_specs=pl.BlockSpec((tm, d), lambda i: (i, 0)),
        out_shape=jax.ShapeDtypeStruct((n, d), F32),
        compiler_params=_cparams(("arbitrary",)),
        name="moe_combine",
    )(x1, y_pairs, gf, lng, lnb)


def _slot_tables(e_idx, gate_w, n):
    nk = n * TOP_K
    e_flat = e_idx.reshape(nk)
    w_flat = gate_w.reshape(nk)
    order = jnp.argsort(e_flat).astype(jnp.int32)
    counts = jnp.bincount(e_flat, length=N_EXPERTS).astype(jnp.int32)
    starts = jnp.cumsum(counts) - counts
    padded = ((counts + MOE_BLOCK - 1) // MOE_BLOCK) * MOE_BLOCK
    pends = jnp.cumsum(padded)
    pstarts = pends - padded
    p = nk + N_EXPERTS * MOE_BLOCK
    nblk = p // MOE_BLOCK
    blk_expert = jnp.minimum(
        jnp.searchsorted(pends, jnp.arange(nblk, dtype=jnp.int32) * MOE_BLOCK, side='right'),
        N_EXPERTS - 1).astype(jnp.int32)
    slot_e = jnp.repeat(blk_expert, MOE_BLOCK)
    rank = jnp.arange(p, dtype=jnp.int32) - pstarts[slot_e]
    is_pad = rank >= counts[slot_e]
    slot_flat = order[jnp.minimum(starts[slot_e] + rank, nk - 1)]
    slot_w = jnp.where(is_pad, 0.0, w_flat[slot_flat])
    slot_tok = jnp.where(is_pad, 0, slot_flat // TOP_K)
    slot_dst = jnp.where(is_pad, nk - 1 + jnp.cumsum(is_pad.astype(jnp.int32)), slot_flat)
    n_used = (pends[-1] // MOE_BLOCK).astype(jnp.int32).reshape(1)
    return slot_tok, slot_dst, blk_expert, n_used, slot_w.reshape(p, 1)


def kernel(x, c, w_in, b_forget, w_attn_o, w_pool, pool_scale, conv_w, conv_b, dt_bias, a_log,
           d_skip, ssm_norm_w, w_ssm_o, w_out, w_ada, b_ada, ln_mix_g, ln_mix_b, ln_ffn_g,
           ln_ffn_b, w_router, b_router, w_exp_gate, w_exp_up, w_exp_down):
    bsz, seq, d = x.shape
    n = bsz * seq
    mod = _ada_call(c, w_ada, b_ada)
    wr_hi, wr_lo = _split_bf16(w_router.T)
    br_col = b_router.reshape(N_EXPERTS, 1)
    x2 = x.reshape(n, d)
    for l in range(DEPTH):
        sh_m, sc_m, g_m, sh_f, sc_f, g_f = (mod[l, :, i * d:(i + 1) * d].reshape(bsz, 1, d)
                                            for i in range(6))
        wl = w_in[l]
        o_f = 3 * ATT_WIDTH
        o_pool = o_f + ATT_HEADS
        o_z = o_pool + POOL_WIDTH
        o_xbc = o_z + SSM_INNER
        o_dt = o_xbc + SSM_CONV_DIM
        o_gate = o_dt + SSM_HEADS
        w_big = jnp.concatenate([wl[:, :o_f], wl[:, o_pool:o_dt], wl[:, o_gate:]], axis=1).astype(BF16)
        w_small = jnp.concatenate(
            [wl[:, o_dt:o_gate], wl[:, o_f:o_pool],
             jnp.zeros((d, SMALL_W - SSM_HEADS - ATT_HEADS), F32)], axis=1)
        ws_hi, ws_lo = _split_bf16(w_small)
        big, small = _inproj_call(x2, sh_m, sc_m, w_big, ws_hi, ws_lo, seq)

        bf_row = jnp.zeros((1, SMALL_W), F32).at[0, F_COL0:F_COL0 + ATT_HEADS].set(b_forget[l])
        fcol = _fcum_call(small, bf_row, bsz, seq)
        y_att = _attn_call(big.reshape(bsz, seq, BIG_W), fcol).reshape(n, ATT_WIDTH)

        pad_heads = jnp.zeros((SMALL_W - SSM_HEADS,), F32)
        dtb_row = jnp.concatenate([dt_bias[l], pad_heads]).reshape(1, SMALL_W)
        alog_row = jnp.concatenate([a_log[l], pad_heads]).reshape(1, SMALL_W)
        dsk_row = jnp.repeat(d_skip[l], SSM_HEAD_DIM).reshape(1, SSM_INNER)
        y_ssm = _ssd_call(big, small, conv_w[l], conv_b[l].reshape(1, SSM_CONV_DIM), dtb_row,
                          alog_row, dsk_row, ssm_norm_w[l].reshape(1, SSM_INNER), bsz, seq)

        x1, h2, logits_t = _mixout_call(
            x2, y_att, y_ssm, big, g_m, sh_f, sc_f, w_attn_o[l].astype(BF16),
            w_ssm_o[l].astype(BF16), w_pool[l].astype(BF16), pool_scale[l].reshape(1, d),
            w_out[l].astype(BF16), ln_mix_g[l].reshape(1, d), ln_mix_b[l].reshape(1, d),
            wr_hi, wr_lo, seq)

        e_idx_t, gate_t = _router_call(logits_t, br_col)
        slot_tok, slot_dst, blk_expert, n_used, slot_w = _slot_tables(e_idx_t.T, gate_t.T, n)
        y_pairs = _expert_call(slot_tok, slot_dst, blk_expert, n_used, h2,
                               w_exp_gate[l].astype(BF16), w_exp_up[l].astype(BF16),
                               w_exp_down[l].astype(BF16), slot_w)
        x2 = _combine_call(x1, y_pairs, g_f, ln_ffn_g[l].reshape(1, d), ln_ffn_b[l].reshape(1, d), seq)
    return x2.reshape(bsz, seq, d)
```

```python
import functools

import jax
import jax.numpy as jnp
from jax import lax
from jax.experimental import pallas as pl
from jax.experimental.pallas import tpu as pltpu

F32 = jnp.float32
BF16 = jnp.bfloat16

D_MODEL = 1024
DEPTH = 2
ATT_HEADS = 16
ATT_HEAD_DIM = 64
ATT_WIDTH = ATT_HEADS * ATT_HEAD_DIM
POOL_WINDOWS = (2, 4, 8, 16)
POOL_GROUPS = len(POOL_WINDOWS)
POOL_WIDTH = D_MODEL
POOL_GROUP_DIM = POOL_WIDTH // POOL_GROUPS
SSM_INNER = 2 * D_MODEL
SSM_HEAD_DIM = 64
SSM_HEADS = SSM_INNER // SSM_HEAD_DIM
SSM_GROUPS = 4
SSM_STATE = 128
SSM_CONV = 4
SSM_CONV_DIM = SSM_INNER + 2 * SSM_GROUPS * SSM_STATE
SSM_GROUP_WIDTH = SSM_INNER // SSM_GROUPS
N_BRANCHES = 3
N_EXPERTS = 16
N_EXPERT_GROUPS = 4
EXPERTS_PER_GROUP = N_EXPERTS // N_EXPERT_GROUPS
TOP_K = 2
EXPERT_DFF = 512
MOE_BLOCK = 256
DEEPNORM_ALPHA = (2 * DEPTH) ** 0.25
LN_EPS = 1e-5
RMS_EPS = 1e-6

LANES = 128
SSD_CHUNK = 64
SMALL_W = LANES
F_COL0 = SSM_HEADS
COL_Q, COL_K, COL_V = 0, ATT_WIDTH, 2 * ATT_WIDTH
COL_POOL = 3 * ATT_WIDTH
COL_Z = COL_POOL + POOL_WIDTH
COL_XBC = COL_Z + SSM_INNER
COL_GATE = COL_XBC + SSM_CONV_DIM
BIG_W = COL_GATE + N_BRANCHES * D_MODEL
VMEM_LIMIT = 48 * 1024 * 1024


def _cparams(sem):
    return pltpu.CompilerParams(dimension_semantics=sem, vmem_limit_bytes=VMEM_LIMIT)


def _sigmoid(v):
    return 0.5 + 0.5 * jnp.tanh(0.5 * v)


def _silu(v):
    return v * _sigmoid(v)


def _softplus(v):
    return jnp.maximum(v, 0.0) + jnp.log1p(jnp.exp(-jnp.abs(v)))


def _log_sigmoid(v):
    return jnp.minimum(v, 0.0) - jnp.log1p(jnp.exp(-jnp.abs(v)))


def _layer_norm(v):
    mu = jnp.mean(v, axis=-1, keepdims=True)
    vc = v - mu
    return vc * lax.rsqrt(jnp.mean(vc * vc, axis=-1, keepdims=True) + LN_EPS)


def _split_bf16(v):
    hi = v.astype(BF16)
    lo = (v - hi.astype(F32)).astype(BF16)
    return hi, lo


def _dot(a, b):
    return jnp.dot(a, b, preferred_element_type=F32)


def _dot3(a_hi, a_lo, b_hi, b_lo):
    return _dot(a_hi, b_hi) + (_dot(a_lo, b_hi) + _dot(a_hi, b_lo))


def _store_token_major(ref, v, group=0, groups=1):
    t, width = v.shape
    c = width // LANES
    for s in range(c):
        ref[pl.ds(group * c + s, t, stride=groups * c), :] = v[:, s * LANES:(s + 1) * LANES]


def _load_token_major(ref, t, c, group=0, groups=1):
    return jnp.concatenate(
        [ref[pl.ds(group * c + s, t, stride=groups * c), :] for s in range(c)], axis=-1)


def _cumsum_rows(v, rows):
    t = v.shape[0]
    k = 1
    while k < t:
        v = v + jnp.where(rows >= k, pltpu.roll(v, k, 0), 0.0)
        k *= 2
    return v


def _ada_kernel(c_ref, w_ref, b_ref, o_ref):
    cond = _silu(c_ref[...])
    c_hi, c_lo = _split_bf16(cond)
    w_hi, w_lo = _split_bf16(w_ref[0])
    o_ref[0] = _dot3(c_hi, c_lo, w_hi, w_lo) + b_ref[0]


def _ada_call(c, w_ada, b_ada):
    depth, d, width = w_ada.shape
    bsz = c.shape[0]
    rows = 8
    c_pad = jnp.zeros((rows, d), F32).at[:bsz].set(c)
    tn = 1024
    out = pl.pallas_call(
        _ada_kernel,
        grid=(depth, width // tn),
        in_specs=[
            pl.BlockSpec((rows, d), lambda l, j: (0, 0)),
            pl.BlockSpec((1, d, tn), lambda l, j: (l, 0, j)),
            pl.BlockSpec((1, 1, tn), lambda l, j: (l, 0, j)),
        ],
        out_specs=pl.BlockSpec((1, rows, tn), lambda l, j: (l, 0, j)),
        out_shape=jax.ShapeDtypeStruct((depth, rows, width), F32),
        compiler_params=_cparams(("arbitrary", "arbitrary")),
        name="ada_mod",
    )(c_pad, w_ada, b_ada.reshape(depth, 1, width))
    return out[:, :bsz]


def _inproj_kernel(x_ref, sh_ref, sc_ref, w_ref, wsh_ref, wsl_ref, big_ref, small_ref, h_ref):
    @pl.when(pl.program_id(1) == 0)
    def _():
        h = _layer_norm(x_ref[...]) * (1.0 + sc_ref[0]) + sh_ref[0]
        h_hi, h_lo = _split_bf16(h)
        h_ref[...] = h_hi
        small_ref[...] = _dot3(h_hi, h_lo, wsh_ref[...], wsl_ref[...])

    big_ref[...] = _dot(h_ref[...], w_ref[...]).astype(BF16)


def _inproj_call(x2, shift, scale, w_big, ws_hi, ws_lo, seq):
    n, d = x2.shape
    tm, tn = min(1024, seq), 1024
    tiles_per_seq = seq // tm
    return pl.pallas_call(
        _inproj_kernel,
        grid=(n // tm, BIG_W // tn),
        in_specs=[
            pl.BlockSpec((tm, d), lambda i, j: (i, 0)),
            pl.BlockSpec((1, 1, d), lambda i, j: (i // tiles_per_seq, 0, 0)),
            pl.BlockSpec((1, 1, d), lambda i, j: (i // tiles_per_seq, 0, 0)),
            pl.BlockSpec((d, tn), lambda i, j: (0, j)),
            pl.BlockSpec((d, SMALL_W), lambda i, j: (0, 0)),
            pl.BlockSpec((d, SMALL_W), lambda i, j: (0, 0)),
        ],
        out_specs=[
            pl.BlockSpec((tm, tn), lambda i, j: (i, j)),
            pl.BlockSpec((tm, SMALL_W), lambda i, j: (i, 0)),
        ],
        out_shape=[
            jax.ShapeDtypeStruct((n, BIG_W), BF16),
            jax.ShapeDtypeStruct((n, SMALL_W), F32),
        ],
        scratch_shapes=[pltpu.VMEM((tm, d), BF16)],
        compiler_params=_cparams(("arbitrary", "arbitrary")),
        name="inproj",
    )(x2, shift, scale, w_big, ws_hi, ws_lo)


def _fcum_kernel(s_ref, bf_ref, o_ref, carry_ref):
    @pl.when(pl.program_id(1) == 0)
    def _():
        carry_ref[...] = jnp.zeros_like(carry_ref)

    lf = _log_sigmoid(s_ref[...] + bf_ref[...])
    rows = lax.broadcasted_iota(jnp.int32, lf.shape, 0)
    out = _cumsum_rows(lf, rows) + carry_ref[0:1, :]
    o_ref[...] = out
    carry_ref[0:1, :] = out[lf.shape[0] - 1:, :]


def _fcum_call(small, bf_row, bsz, seq):
    t = 512
    per = seq // t
    return pl.pallas_call(
        _fcum_kernel,
        grid=(bsz, per),
        in_specs=[
            pl.BlockSpec((t, SMALL_W), lambda b, i: (b * per + i, 0)),
            pl.BlockSpec((1, SMALL_W), lambda b, i: (0, 0)),
        ],
        out_specs=pl.BlockSpec((t, SMALL_W), lambda b, i: (b * per + i, 0)),
        out_shape=jax.ShapeDtypeStruct(small.shape, F32),
        scratch_shapes=[pltpu.VMEM((8, SMALL_W), F32)],
        compiler_params=_cparams(("arbitrary", "arbitrary")),
        name="forget_cumsum",
    )(small, bf_row)


def _split3_bf16(v):
    hi = v.astype(BF16)
    r = v - hi.astype(F32)
    mid = r.astype(BF16)
    lo = (r - mid.astype(F32)).astype(BF16)
    return hi, mid, lo


def _attn_kernel(q_ref, k_ref, v_ref, f_ref, o_ref, kx_ref, vx_ref, *, tq, tk):
    hp = pl.program_id(1)
    qi = pl.program_id(2)
    dh = ATT_HEAD_DIM
    seq = k_ref.shape[0]

    def head_lanes(rows, hh):
        lane = lax.broadcasted_iota(jnp.int32, (rows, LANES), 1)
        in_head = (lane >= hh * dh) & (lane < (hh + 1) * dh)
        return lane - (1 - hh) * dh, in_head

    def f_column(f_tile, hh):
        lane = lax.broadcasted_iota(jnp.int32, f_tile.shape, 1)
        return jnp.sum(jnp.where(lane == F_COL0 + 2 * hp + hh, f_tile, 0.0), axis=-1, keepdims=True)

    def f_terms(f_tile, hh):
        return [t.astype(F32) for t in _split3_bf16(f_column(f_tile, hh))]

    @pl.when(qi == 0)
    def _():
        f_all = f_ref[...]
        for hh in range(2):
            ext, in_head = head_lanes(seq, hh)
            hi, mid, lo = f_terms(f_all, hh)
            k_sp = jnp.where((ext >= 0) & (ext < 3), 1.0,
                             jnp.where(ext == 3, -hi, jnp.where(ext == 4, -mid,
                                                                jnp.where(ext == 5, -lo, 0.0))))
            kx_ref[hh] = jnp.where(in_head, k_ref[...], k_sp.astype(BF16))
            vx_ref[hh] = jnp.where(in_head, v_ref[...], jnp.where(ext == 0, 1.0, 0.0).astype(BF16))

    f_q = f_ref[pl.ds(pl.multiple_of(qi * tq, tq), tq), :]
    r_iota = lax.broadcasted_iota(jnp.int32, (tk, tk), 0)
    c_iota = lax.broadcasted_iota(jnp.int32, (tk, tk), 1)
    sub = tq // tk
    nt = (((1,), (1,)), ((), ()))
    qx = []
    for hh in range(2):
        ext, in_head = head_lanes(tq, hh)
        hi, mid, lo = f_terms(f_q, hh)
        q_sp = jnp.where(ext == 0, hi, jnp.where(ext == 1, mid, jnp.where(ext == 2, lo,
                         jnp.where((ext >= 3) & (ext < 6), 1.0, 0.0))))
        qx.append(jnp.where(in_head, q_ref[...] * 0.125, q_sp.astype(BF16)))

    def block(hh, j, state, diag):
        off = pl.multiple_of(j * tk, tk)
        first = 0 if diag is None else diag
        s = lax.dot_general(qx[hh][first * tk:], kx_ref[hh, pl.ds(off, tk), :], nt,
                            preferred_element_type=F32)
        new_m, probs = [], []
        for r in range(first, sub):
            s_r = s[(r - first) * tk:(r - first + 1) * tk]
            if diag is not None and r == diag:
                s_r = jnp.where(r_iota >= c_iota, s_r, -jnp.inf)
            s_max = jnp.max(s_r, axis=-1, keepdims=True)
            m_r = s_max if state[r] is None else jnp.maximum(state[r][0], s_max)
            new_m.append(m_r)
            probs.append(jnp.exp((s_r - m_r).astype(BF16)))
        pv = _dot(probs[0] if len(probs) == 1 else jnp.concatenate(probs, axis=0),
                  vx_ref[hh, pl.ds(off, tk), :])
        out = list(state)
        for r in range(first, sub):
            pv_r = pv[(r - first) * tk:(r - first + 1) * tk]
            m_r = new_m[r - first]
            out[r] = (m_r, pv_r if state[r] is None
                      else jnp.exp(state[r][0] - m_r) * state[r][1] + pv_r)
        return out

    st0 = st1 = [None] * sub
    for dg in range(sub):
        st0 = block(0, qi * sub + dg, st0, dg)
        st1 = block(1, qi * sub + dg, st1, dg)

    def body(j, carry):
        st0, st1 = carry
        return block(0, j, st0, None), block(1, j, st1, None)

    st0, st1 = lax.fori_loop(0, qi * sub, body, (st0, st1))
    a0 = jnp.concatenate([acc for _, acc in st0], axis=0) if sub > 1 else st0[0][1]
    a1 = jnp.concatenate([acc for _, acc in st1], axis=0) if sub > 1 else st1[0][1]
    out = None
    for hh, acc in ((0, a0), (1, a1)):
        ext, in_head = head_lanes(tq, hh)
        denom = jnp.sum(jnp.where(ext == 0, acc, 0.0), axis=-1, keepdims=True)
        o = acc / denom
        out = o if out is None else jnp.where(in_head, o, out)
    o_ref[...] = out.astype(BF16)


def _attn_call(big3, fcol):
    bsz, seq, _ = big3.shape
    tq = min(1024, seq)
    tk = min(512, tq)
    nq = seq // tq
    pairs = ATT_HEADS // 2
    kb, vb = COL_K // LANES, COL_V // LANES
    return pl.pallas_call(
        functools.partial(_attn_kernel, tq=tq, tk=tk),
        grid=(bsz, pairs, nq),
        in_specs=[
            pl.BlockSpec((None, tq, LANES), lambda b, h, i: (b, i, h)),
            pl.BlockSpec((None, seq, LANES), lambda b, h, i: (b, 0, kb + h)),
            pl.BlockSpec((None, seq, LANES), lambda b, h, i: (b, 0, vb + h)),
            pl.BlockSpec((seq, SMALL_W), lambda b, h, i: (b, 0)),
        ],
        out_specs=pl.BlockSpec((None, tq, LANES), lambda b, h, i: (b, i, h)),
        out_shape=jax.ShapeDtypeStruct((bsz, seq, ATT_WIDTH), BF16),
        scratch_shapes=[pltpu.VMEM((2, seq, LANES), BF16), pltpu.VMEM((2, seq, LANES), BF16)],
        compiler_params=_cparams(("arbitrary", "arbitrary", "arbitrary")),
        name="fox_attention",
    )(big3, big3, big3, fcol)


def _ssd_kernel(z_ref, xbc_ref, dt_ref, cw_ref, cb_ref, dtb_ref, alog_ref, dsk_ref, nw_ref,
                y_ref, ext_ref, state_ref):
    L = SSD_CHUNK
    P = SSM_HEAD_DIM
    GW = SSM_GROUP_WIDTH
    NS = SSM_STATE
    halo = 8

    @pl.when(pl.program_id(1) == 0)
    def _():
        ext_ref[0:halo, :] = jnp.zeros((halo, SSM_CONV_DIM), F32)
        state_ref[...] = jnp.zeros_like(state_ref)

    ext_ref[halo:halo + L, :] = xbc_ref[...].astype(F32)
    ext = ext_ref[...]
    conv = cb_ref[...] + cw_ref[SSM_CONV - 1:SSM_CONV, :] * ext[halo:, :]
    for lag in range(1, SSM_CONV):
        j = SSM_CONV - 1 - lag
        conv = conv + cw_ref[j:j + 1, :] * pltpu.roll(ext, lag, 0)[halo:, :]
    ext_ref[0:halo, :] = ext[L:L + halo, :]
    xc = _silu(conv)

    rows = lax.broadcasted_iota(jnp.int32, (L, LANES), 0)
    lane = lax.broadcasted_iota(jnp.int32, (L, LANES), 1)
    lo_half = lane < P
    dt = _softplus(dt_ref[...] + dtb_ref[...])
    a_cs = _cumsum_rows(dt * (-jnp.exp(alog_ref[...])), rows)
    a_t = jnp.concatenate([a_cs, a_cs], axis=0).T
    causal = rows >= jnp.where(lo_half, lane, lane - P)
    blk = (lax.broadcasted_iota(jnp.int32, (2 * L, LANES), 0) < L) == \
          (lax.broadcasted_iota(jnp.int32, (2 * L, LANES), 1) < P)

    def pair_cols(mat, h0):
        r = mat.shape[0]
        c0 = jnp.broadcast_to(mat[:, h0:h0 + 1], (r, LANES))
        c1 = jnp.broadcast_to(mat[:, h0 + 1:h0 + 2], (r, LANES))
        return jnp.where(lo_half[:r], c0, c1)

    def pair_rows(mat_t, h0):
        return jnp.where(lo_half[:1], mat_t[h0:h0 + 1, :], mat_t[h0 + 1:h0 + 2, :])

    for g in range(SSM_GROUPS):
        bm = xc[:, SSM_INNER + g * NS:SSM_INNER + (g + 1) * NS]
        cm = xc[:, SSM_INNER + SSM_GROUPS * NS + g * NS:SSM_INNER + SSM_GROUPS * NS + (g + 1) * NS]
        cm_b = cm.astype(BF16)
        bm_t2 = jnp.concatenate([bm, bm], axis=0).T.astype(BF16)
        cb2 = _dot(cm_b, bm_t2)
        st_prev = state_ref[g]
        y_off = _dot(cm_b, st_prev.astype(BF16))
        y_parts, xw_parts, cd_parts = [], [], []
        for pr in range(GW // LANES):
            h0 = g * (GW // P) + 2 * pr
            c0 = g * GW + pr * LANES
            xs_p = xc[:, c0:c0 + LANES]
            a_col = pair_cols(a_cs, h0)
            a_end = a_col[L - 1:L, :]
            seg = a_col - pair_rows(a_t, h0)
            decay = jnp.where(causal, jnp.exp(jnp.where(causal, seg, 0.0)), 0.0)
            xdt = xs_p * pair_cols(dt, h0)
            xdt2 = jnp.concatenate([xdt, xdt], axis=0)
            xdt_bd = jnp.where(blk, xdt2, 0.0).astype(BF16)
            y_d = _dot((cb2 * decay).astype(BF16), xdt_bd)
            y_o = y_off[:, pr * LANES:(pr + 1) * LANES] * jnp.exp(a_col)
            y_parts.append(y_d + y_o + dsk_ref[:, c0:c0 + LANES] * xs_p)
            xw_parts.append(xdt * jnp.exp(a_end - a_col))
            cd_parts.append(jnp.exp(a_end))
        xw = jnp.concatenate(xw_parts, axis=-1).astype(BF16)
        state_ref[g] = st_prev * jnp.concatenate(cd_parts, axis=-1) + _dot(bm_t2[:, :L], xw)
        y = jnp.concatenate(y_parts, axis=-1) * _silu(z_ref[:, g * GW:(g + 1) * GW].astype(F32))
        y = y * lax.rsqrt(jnp.mean(y * y, axis=-1, keepdims=True) + RMS_EPS)
        y_ref[:, g * GW:(g + 1) * GW] = (y * nw_ref[:, g * GW:(g + 1) * GW]).astype(BF16)


def _ssd_call(big, small, conv_w, conv_b, dtb_row, alog_row, dsk_row, nw_row, bsz, seq):
    n = big.shape[0]
    L = SSD_CHUNK
    nc = seq // L
    const = lambda b, c: (0, 0)
    return pl.pallas_call(
        _ssd_kernel,
        grid=(bsz, nc),
        in_specs=[
            pl.BlockSpec((L, SSM_INNER), lambda b, c: (b * nc + c, COL_Z // SSM_INNER)),
            pl.BlockSpec((L, SSM_CONV_DIM), lambda b, c: (b * nc + c, COL_XBC // SSM_CONV_DIM)),
            pl.BlockSpec((L, SMALL_W), lambda b, c: (b * nc + c, 0)),
            pl.BlockSpec((SSM_CONV, SSM_CONV_DIM), const),
            pl.BlockSpec((1, SSM_CONV_DIM), const),
            pl.BlockSpec((1, SMALL_W), const),
            pl.BlockSpec((1, SMALL_W), const),
            pl.BlockSpec((1, SSM_INNER), const),
            pl.BlockSpec((1, SSM_INNER), const),
        ],
        out_specs=pl.BlockSpec((L, SSM_INNER), lambda b, c: (b * nc + c, 0)),
        out_shape=jax.ShapeDtypeStruct((n, SSM_INNER), BF16),
        scratch_shapes=[
            pltpu.VMEM((8 + L, SSM_CONV_DIM), F32),
            pltpu.VMEM((SSM_GROUPS, SSM_STATE, SSM_GROUP_WIDTH), F32),
        ],
        compiler_params=_cparams(("arbitrary", "arbitrary")),
        name="ssd_scan",
    )(big, big, small, conv_w, conv_b, dtb_row, alog_row, dsk_row, nw_row)


def _mixout_kernel(x_ref, ya_ref, ys_ref, u_ref, uh_ref, gate_ref, gm_ref, shf_ref, scf_ref,
                   wao_ref, wso_ref, wp_ref, ps_ref, wout_ref, lng_ref, lnb_ref, wrh_ref, wrl_ref,
                   x1_ref, h2_ref, lt_ref, ext_ref, *, tm, tiles_per_seq):
    i = pl.program_id(0)
    halo = 16
    first = (i % tiles_per_seq) == 0
    ext_ref[0:halo, :] = jnp.where(first, 0.0, uh_ref[...].astype(F32))
    u = u_ref[...].astype(F32)
    ext_ref[halo:halo + tm, :] = u
    pos = ((i % tiles_per_seq) * tm + 1 + lax.broadcasted_iota(jnp.int32, (tm, 1), 0)).astype(F32)

    y_att = _dot(ya_ref[...], wao_ref[...])
    y_ssm = _dot(ys_ref[...], wso_ref[...])
    pool_parts = []
    for g, w in enumerate(POOL_WINDOWS):
        c0 = g * POOL_GROUP_DIM
        s = ext_ref[:, c0:c0 + POOL_GROUP_DIM]
        k = 1
        while k < w:
            s = s + pltpu.roll(s, k, 0)
            k *= 2
        pooled = s[halo:, :] / jnp.minimum(pos, float(w)) - u[:, c0:c0 + POOL_GROUP_DIM]
        pool_parts.append(_dot(pooled.astype(BF16), wp_ref[g]))
    y_pool = jnp.concatenate(pool_parts, axis=-1) * ps_ref[...]

    g_att = _sigmoid(gate_ref[:, 0:D_MODEL].astype(F32))
    g_pool = _sigmoid(gate_ref[:, D_MODEL:2 * D_MODEL].astype(F32))
    g_ssm = _sigmoid(gate_ref[:, 2 * D_MODEL:3 * D_MODEL].astype(F32))
    merged = g_att * y_att + g_pool * y_pool + g_ssm * y_ssm
    mix = _dot(merged.astype(BF16), wout_ref[...])
    x1 = _layer_norm(DEEPNORM_ALPHA * x_ref[...] + gm_ref[0] * mix) * lng_ref[...] + lnb_ref[...]
    x1_ref[...] = x1
    h2 = _layer_norm(x1) * (1.0 + scf_ref[0]) + shf_ref[0]
    _store_token_major(h2_ref, h2)
    h_hi, h_lo = _split_bf16(h2)
    nt = (((1,), (1,)), ((), ()))
    lt_ref[...] = (lax.dot_general(wrh_ref[...], h_hi, nt, preferred_element_type=F32)
                   + (lax.dot_general(wrh_ref[...], h_lo, nt, preferred_element_type=F32)
                      + lax.dot_general(wrl_ref[...], h_hi, nt, preferred_element_type=F32)))


def _mixout_call(x2, y_att, y_ssm, big, gm, shf, scf, wao, wso, wp, ps_row, wout, lng, lnb,
                 wr_hi, wr_lo, seq):
    n, d = x2.shape
    tm = 256
    halo = 16
    tiles_per_seq = seq // tm
    const2 = lambda i: (0, 0)
    per_seq = lambda i: (i // tiles_per_seq, 0, 0)
    return pl.pallas_call(
        functools.partial(_mixout_kernel, tm=tm, tiles_per_seq=tiles_per_seq),
        grid=(n // tm,),
        in_specs=[
            pl.BlockSpec((tm, d), lambda i: (i, 0)),
            pl.BlockSpec((tm, ATT_WIDTH), lambda i: (i, 0)),
            pl.BlockSpec((tm, SSM_INNER), lambda i: (i, 0)),
            pl.BlockSpec((tm, POOL_WIDTH), lambda i: (i, COL_POOL // POOL_WIDTH)),
            pl.BlockSpec((halo, POOL_WIDTH),
                         lambda i: (jnp.maximum(i * (tm // halo) - 1, 0), COL_POOL // POOL_WIDTH)),
            pl.BlockSpec((tm, N_BRANCHES * d), lambda i: (i, COL_GATE // (N_BRANCHES * d))),
            pl.BlockSpec((1, 1, d), per_seq),
            pl.BlockSpec((1, 1, d), per_seq),
            pl.BlockSpec((1, 1, d), per_seq),
            pl.BlockSpec((ATT_WIDTH, d), const2),
            pl.BlockSpec((SSM_INNER, d), const2),
            pl.BlockSpec((POOL_GROUPS, POOL_GROUP_DIM, POOL_GROUP_DIM), lambda i: (0, 0, 0)),
            pl.BlockSpec((1, d), const2),
            pl.BlockSpec((d, d), const2),
            pl.BlockSpec((1, d), const2),
            pl.BlockSpec((1, d), const2),
            pl.BlockSpec((N_EXPERTS, d), const2),
            pl.BlockSpec((N_EXPERTS, d), const2),
        ],
        out_specs=[
            pl.BlockSpec((tm, d), lambda i: (i, 0)),
            pl.BlockSpec((tm * (d // LANES), LANES), lambda i: (i, 0)),
            pl.BlockSpec((N_EXPERTS, tm), lambda i: (0, i)),
        ],
        out_shape=[
            jax.ShapeDtypeStruct((n, d), F32),
            jax.ShapeDtypeStruct((n * (d // LANES), LANES), F32),
            jax.ShapeDtypeStruct((N_EXPERTS, n), F32),
        ],
        scratch_shapes=[pltpu.VMEM((halo + tm, POOL_WIDTH), F32)],
        compiler_params=_cparams(("arbitrary",)),
        name="mixer_out",
    )(x2, y_att, y_ssm, big, big, big, gm, shf, scf, wao, wso, wp, ps_row, wout, lng, lnb,
      wr_hi, wr_lo)


def _top2(vals):
    n = len(vals)
    v1 = vals[0]
    for v in vals[1:]:
        v1 = jnp.maximum(v1, v)
    i1 = jnp.full(v1.shape, n, jnp.int32)
    for j in reversed(range(n)):
        i1 = jnp.where(vals[j] == v1, j, i1)
    v2 = jnp.full(v1.shape, -jnp.inf, F32)
    for j in range(n):
        v2 = jnp.maximum(v2, jnp.where(i1 == j, -jnp.inf, vals[j]))
    i2 = jnp.full(v1.shape, n, jnp.int32)
    for j in reversed(range(n)):
        i2 = jnp.where((vals[j] == v2) & (i1 != j), j, i2)
    return v1, i1, v2, i2


def _router_kernel(lt_ref, br_ref, e_ref, w_ref, cnt_ref):
    lg = lt_ref[...]
    m = jnp.max(lg, axis=0, keepdims=True)
    ex = jnp.exp(lg - m)
    probs = ex / jnp.sum(ex, axis=0, keepdims=True)
    sel = probs + br_ref[...]
    tops = []
    for g in range(N_EXPERT_GROUPS):
        vals = [sel[g * EXPERTS_PER_GROUP + j:g * EXPERTS_PER_GROUP + j + 1, :]
                for j in range(EXPERTS_PER_GROUP)]
        tops.append(_top2(vals))
    best = tops[0][0] + tops[0][2]
    e1 = tops[0][1]
    e2 = tops[0][3]
    for g in range(1, N_EXPERT_GROUPS):
        score = tops[g][0] + tops[g][2]
        better = score > best
        best = jnp.where(better, score, best)
        e1 = jnp.where(better, tops[g][1] + g * EXPERTS_PER_GROUP, e1)
        e2 = jnp.where(better, tops[g][3] + g * EXPERTS_PER_GROUP, e2)
    p1 = jnp.zeros_like(best)
    p2 = jnp.zeros_like(best)
    for e in range(N_EXPERTS):
        p1 = jnp.where(e1 == e, probs[e:e + 1, :], p1)
        p2 = jnp.where(e2 == e, probs[e:e + 1, :], p2)
    tot = p1 + p2
    e_ref[...] = jnp.concatenate([e1, e2], axis=0)
    w_ref[...] = jnp.concatenate([p1 / tot, p2 / tot], axis=0)

    @pl.when(pl.program_id(0) == 0)
    def _():
        cnt_ref[...] = jnp.zeros_like(cnt_ref)

    e_rows = lax.broadcasted_iota(jnp.int32, lg.shape, 0)
    hits = ((e_rows == e1) | (e_rows == e2)).astype(jnp.int32)
    cnt_ref[...] += jnp.sum(hits, axis=1, keepdims=True)


def _router_call(logits_t, br_col):
    e, n = logits_t.shape
    t = min(2048, n)
    return pl.pallas_call(
        _router_kernel,
        grid=(n // t,),
        in_specs=[
            pl.BlockSpec((e, t), lambda i: (0, i)),
            pl.BlockSpec((e, 1), lambda i: (0, 0)),
        ],
        out_specs=[
            pl.BlockSpec((TOP_K, t), lambda i: (0, i)),
            pl.BlockSpec((TOP_K, t), lambda i: (0, i)),
            pl.BlockSpec((e, LANES), lambda i: (0, 0)),
        ],
        out_shape=[
            jax.ShapeDtypeStruct((TOP_K, n), jnp.int32),
            jax.ShapeDtypeStruct((TOP_K, n), F32),
            jax.ShapeDtypeStruct((e, LANES), jnp.int32),
        ],
        compiler_params=_cparams(("arbitrary",)),
        name="router_top2",
    )(logits_t, br_col)


def _expert_kernel(tok_ref, dst_ref, bexp_ref, h_hbm,
                   w1a_ref, w3a_ref, w2a_ref, swa_ref, w1b_ref, w3b_ref, w2b_ref, swb_ref,
                   y_hbm, xa, xb, ya, yb, gsem, ssem):
    i = pl.program_id(0)
    last = pl.num_programs(0) - 1
    c = w1a_ref.shape[1] // LANES

    def start_gather(blk, xbuf, sem):
        base = blk * MOE_BLOCK
        for k in range(MOE_BLOCK):
            row = pl.multiple_of(tok_ref[base + k], c)
            pltpu.make_async_copy(h_hbm.at[pl.ds(row, c), :], xbuf.at[pl.ds(k * c, c), :], sem).start()

    def start_scatter(blk, ybuf, sem):
        base = (blk + 1) * MOE_BLOCK
        for k in range(MOE_BLOCK):
            row = pl.multiple_of(dst_ref[base + k], c)
            pltpu.make_async_copy(ybuf.at[pl.ds(k * c, c), :], y_hbm.at[pl.ds(row, c), :], sem).start()

    def wait_gather(xbuf, sem):
        pltpu.make_async_copy(h_hbm.at[pl.ds(0, MOE_BLOCK * c), :], xbuf, sem).wait()

    def wait_scatter(ybuf, sem):
        pltpu.make_async_copy(ybuf, y_hbm.at[pl.ds(0, MOE_BLOCK * c), :], sem).wait()

    def ffn(xbuf, w1_ref, w3_ref, w2_ref, sw_ref, ybuf):
        xv = _load_token_major(xbuf, MOE_BLOCK, c).astype(BF16)
        hid = _silu(_dot(xv, w1_ref[0])) * _dot(xv, w3_ref[0])
        _store_token_major(ybuf, _dot(hid.astype(BF16), w2_ref[0]) * sw_ref[...])

    @pl.when(i == 0)
    def _():
        yb[...] = jnp.zeros_like(yb)
        start_gather(0, xa, gsem.at[0])

    @pl.when(i > 0)
    def _():
        wait_scatter(ya, ssem.at[0])

    wait_gather(xa, gsem.at[0])
    start_gather(2 * i + 1, xb, gsem.at[1])
    start_scatter(2 * i - 1, yb, ssem.at[1])
    ffn(xa, w1a_ref, w3a_ref, w2a_ref, swa_ref, ya)

    wait_gather(xb, gsem.at[1])
    wait_scatter(yb, ssem.at[1])
    start_gather(2 * i + 2, xa, gsem.at[0])
    start_scatter(2 * i, ya, ssem.at[0])
    ffn(xb, w1b_ref, w3b_ref, w2b_ref, swb_ref, yb)

    @pl.when(i == last)
    def _():
        wait_gather(xa, gsem.at[0])
        start_scatter(2 * i + 1, yb, ssem.at[1])
        wait_scatter(ya, ssem.at[0])
        wait_scatter(yb, ssem.at[1])


def _expert_call(tok_ext, dst_ext, blk_expert, h2_tiles, w1, w3, w2, slot_w):
    d = w1.shape[1]
    c = d // LANES
    p = slot_w.shape[0]
    nblk = p // MOE_BLOCK
    assert nblk % 2 == 0 and tok_ext.shape[0] == p + MOE_BLOCK and dst_ext.shape[0] == p + MOE_BLOCK

    def weight_spec(shape, parity):
        return pl.BlockSpec(shape, lambda i, tok, dst, be: (be[2 * i + parity], 0, 0))

    def block_specs(parity):
        return [weight_spec((1, d, EXPERT_DFF), parity), weight_spec((1, d, EXPERT_DFF), parity),
                weight_spec((1, EXPERT_DFF, d), parity),
                pl.BlockSpec((MOE_BLOCK, 1), lambda i, tok, dst, be: (2 * i + parity, 0))]

    grid_spec = pltpu.PrefetchScalarGridSpec(
        num_scalar_prefetch=3,
        grid=(nblk // 2,),
        in_specs=[pl.BlockSpec(memory_space=pl.ANY)] + block_specs(0) + block_specs(1),
        out_specs=pl.BlockSpec(memory_space=pl.ANY),
        scratch_shapes=[pltpu.VMEM((MOE_BLOCK * c, LANES), F32)] * 4
        + [pltpu.SemaphoreType.DMA((2,)), pltpu.SemaphoreType.DMA((2,))],
    )
    return pl.pallas_call(
        _expert_kernel,
        grid_spec=grid_spec,
        out_shape=jax.ShapeDtypeStruct(((p + MOE_BLOCK) * c, LANES), F32),
        compiler_params=_cparams(("arbitrary",)),
        name="moe_experts",
    )(tok_ext, dst_ext, blk_expert, h2_tiles, w1, w3, w2, slot_w, w1, w3, w2, slot_w)


def _combine_kernel(x_ref, y_ref, gf_ref, lng_ref, lnb_ref, o_ref):
    tm, d = x_ref.shape
    c = d // LANES
    ffn = _load_token_major(y_ref, tm, c, 0, TOP_K) + _load_token_major(y_ref, tm, c, 1, TOP_K)
    o_ref[...] = _layer_norm(DEEPNORM_ALPHA * x_ref[...] + gf_ref[0] * ffn) * lng_ref[...] + lnb_ref[...]


def _combine_call(x1, y_pairs, gf, lng, lnb, seq):
    n, d = x1.shape
    tm = 512
    tiles_per_seq = seq // tm
    return pl.pallas_call(
        _combine_kernel,
        grid=(n // tm,),
        in_specs=[
            pl.BlockSpec((tm, d), lambda i: (i, 0)),
            pl.BlockSpec((tm * TOP_K * (d // LANES), LANES), lambda i: (i, 0)),
            pl.BlockSpec((1, 1, d), lambda i: (i // tiles_per_seq, 0, 0)),
            pl.BlockSpec((1, d), lambda i: (0, 0)),
            pl.BlockSpec((1, d), lambda i: (0, 0)),
        ],
        out_specs=pl.BlockSpec((tm, d), lambda i: (i, 0)),
        out_shape=jax.ShapeDtypeStruct((n, d), F32),
        compiler_params=_cparams(("arbitrary",)),
        name="moe_combine",
    )(x1, y_pairs, gf, lng, lnb)


def _slot_tables(e_idx, gate_w, counts, n, tile_rows):
    nk = n * TOP_K
    e_flat = e_idx.reshape(nk)
    w_flat = gate_w.reshape(nk)
    order = jnp.argsort(e_flat).astype(jnp.int32)
    starts = jnp.cumsum(counts) - counts
    padded = ((counts + MOE_BLOCK - 1) // MOE_BLOCK) * MOE_BLOCK
    pends = jnp.cumsum(padded)
    pstarts = pends - padded
    p = nk + N_EXPERTS * MOE_BLOCK
    nblk = p // MOE_BLOCK
    blk_start = jnp.arange(nblk, dtype=jnp.int32) * MOE_BLOCK
    blk_expert = jnp.minimum(jnp.sum(pends[None, :] <= blk_start[:, None], axis=1),
                             N_EXPERTS - 1).astype(jnp.int32)
    per_slot = lambda tbl: jnp.repeat(tbl[blk_expert], MOE_BLOCK)
    rank = jnp.arange(p, dtype=jnp.int32) - per_slot(pstarts)
    is_pad = rank >= per_slot(counts)
    slot_flat = order[jnp.minimum(per_slot(starts) + rank, nk - 1)]
    slot_w = jnp.where(is_pad, 0.0, w_flat[slot_flat])
    pads_before = per_slot(pstarts - starts) + rank - per_slot(counts)
    slot_tok = jnp.where(is_pad, 0, slot_flat // TOP_K)
    slot_dst = jnp.where(is_pad, nk + pads_before, slot_flat)
    tok_ext = jnp.concatenate([slot_tok, jnp.zeros((MOE_BLOCK,), jnp.int32)])
    dst_ext = jnp.concatenate([p + jnp.arange(MOE_BLOCK, dtype=jnp.int32), slot_dst])
    return tok_ext * tile_rows, dst_ext * tile_rows, blk_expert, slot_w.reshape(p, 1)


def kernel(x, c, w_in, b_forget, w_attn_o, w_pool, pool_scale, conv_w, conv_b, dt_bias, a_log,
           d_skip, ssm_norm_w, w_ssm_o, w_out, w_ada, b_ada, ln_mix_g, ln_mix_b, ln_ffn_g,
           ln_ffn_b, w_router, b_router, w_exp_gate, w_exp_up, w_exp_down):
    bsz, seq, d = x.shape
    n = bsz * seq
    mod = _ada_call(c, w_ada, b_ada)
    wr_hi, wr_lo = _split_bf16(w_router.T)
    br_col = b_router.reshape(N_EXPERTS, 1)
    x2 = x.reshape(n, d)
    for l in range(DEPTH):
        sh_m, sc_m, g_m, sh_f, sc_f, g_f = (mod[l, :, i * d:(i + 1) * d].reshape(bsz, 1, d)
                                            for i in range(6))
        wl = w_in[l]
        o_f = 3 * ATT_WIDTH
        o_pool = o_f + ATT_HEADS
        o_z = o_pool + POOL_WIDTH
        o_xbc = o_z + SSM_INNER
        o_dt = o_xbc + SSM_CONV_DIM
        o_gate = o_dt + SSM_HEADS
        w_big = jnp.concatenate([wl[:, :o_f], wl[:, o_pool:o_dt], wl[:, o_gate:]], axis=1).astype(BF16)
        w_small = jnp.concatenate(
            [wl[:, o_dt:o_gate], wl[:, o_f:o_pool],
             jnp.zeros((d, SMALL_W - SSM_HEADS - ATT_HEADS), F32)], axis=1)
        ws_hi, ws_lo = _split_bf16(w_small)
        big, small = _inproj_call(x2, sh_m, sc_m, w_big, ws_hi, ws_lo, seq)

        bf_row = jnp.zeros((1, SMALL_W), F32).at[0, F_COL0:F_COL0 + ATT_HEADS].set(b_forget[l])
        fcol = _fcum_call(small, bf_row, bsz, seq)
        y_att = _attn_call(big.reshape(bsz, seq, BIG_W), fcol).reshape(n, ATT_WIDTH)

        pad_heads = jnp.zeros((SMALL_W - SSM_HEADS,), F32)
        dtb_row = jnp.concatenate([dt_bias[l], pad_heads]).reshape(1, SMALL_W)
        alog_row = jnp.concatenate([a_log[l], pad_heads]).reshape(1, SMALL_W)
        dsk_row = jnp.repeat(d_skip[l], SSM_HEAD_DIM).reshape(1, SSM_INNER)
        y_ssm = _ssd_call(big, small, conv_w[l], conv_b[l].reshape(1, SSM_CONV_DIM), dtb_row,
                          alog_row, dsk_row, ssm_norm_w[l].reshape(1, SSM_INNER), bsz, seq)

        x1, h2, logits_t = _mixout_call(
            x2, y_att, y_ssm, big, g_m, sh_f, sc_f, w_attn_o[l].astype(BF16),
            w_ssm_o[l].astype(BF16), w_pool[l].astype(BF16), pool_scale[l].reshape(1, d),
            w_out[l].astype(BF16), ln_mix_g[l].reshape(1, d), ln_mix_b[l].reshape(1, d),
            wr_hi, wr_lo, seq)

        e_idx_t, gate_t, counts = _router_call(logits_t, br_col)
        tok_ext, dst_ext, blk_expert, slot_w = _slot_tables(e_idx_t.T, gate_t.T, counts[:, 0], n,
                                                            d // LANES)
        y_pairs = _expert_call(tok_ext, dst_ext, blk_expert, h2,
                               w_exp_gate[l].astype(BF16), w_exp_up[l].astype(BF16),
                               w_exp_down[l].astype(BF16), slot_w)
        x2 = _combine_call(x1, y_pairs, g_f, ln_ffn_g[l].reshape(1, d), ln_ffn_b[l].reshape(1, d), seq)
    return x2.reshape(bsz, seq, d)
```

```python
import functools

import jax
import jax.numpy as jnp
from jax import lax
from jax.experimental import pallas as pl
from jax.experimental.pallas import tpu as pltpu

F32 = jnp.float32
BF16 = jnp.bfloat16

D_MODEL = 1024
DEPTH = 2
ATT_HEADS = 16
ATT_HEAD_DIM = 64
ATT_WIDTH = ATT_HEADS * ATT_HEAD_DIM
POOL_WINDOWS = (2, 4, 8, 16)
POOL_GROUPS = len(POOL_WINDOWS)
POOL_WIDTH = D_MODEL
POOL_GROUP_DIM = POOL_WIDTH // POOL_GROUPS
SSM_INNER = 2 * D_MODEL
SSM_HEAD_DIM = 64
SSM_HEADS = SSM_INNER // SSM_HEAD_DIM
SSM_GROUPS = 4
SSM_STATE = 128
SSM_CONV = 4
SSM_CONV_DIM = SSM_INNER + 2 * SSM_GROUPS * SSM_STATE
SSM_GROUP_WIDTH = SSM_INNER // SSM_GROUPS
N_BRANCHES = 3
N_EXPERTS = 16
N_EXPERT_GROUPS = 4
EXPERTS_PER_GROUP = N_EXPERTS // N_EXPERT_GROUPS
TOP_K = 2
EXPERT_DFF = 512
MOE_BLOCK = 256
DEEPNORM_ALPHA = (2 * DEPTH) ** 0.25
LN_EPS = 1e-5
RMS_EPS = 1e-6

LANES = 128
SSD_CHUNK = 64
DMA_PRIORITIES = 2
SMALL_W = LANES
F_COL0 = SSM_HEADS
COL_Q, COL_K, COL_V = 0, ATT_WIDTH, 2 * ATT_WIDTH
COL_POOL = 3 * ATT_WIDTH
COL_Z = COL_POOL + POOL_WIDTH
COL_XBC = COL_Z + SSM_INNER
COL_GATE = COL_XBC + SSM_CONV_DIM
BIG_W = COL_GATE + N_BRANCHES * D_MODEL
VMEM_LIMIT = 48 * 1024 * 1024


def _cparams(sem):
    return pltpu.CompilerParams(dimension_semantics=sem, vmem_limit_bytes=VMEM_LIMIT)


def _sigmoid(v):
    return 0.5 + 0.5 * jnp.tanh(0.5 * v)


def _silu(v):
    return v * _sigmoid(v)


def _softplus(v):
    return jnp.maximum(v, 0.0) + jnp.log1p(jnp.exp(-jnp.abs(v)))


def _log_sigmoid(v):
    return jnp.minimum(v, 0.0) - jnp.log1p(jnp.exp(-jnp.abs(v)))


def _layer_norm(v):
    mu = jnp.mean(v, axis=-1, keepdims=True)
    vc = v - mu
    return vc * lax.rsqrt(jnp.mean(vc * vc, axis=-1, keepdims=True) + LN_EPS)


def _split_bf16(v):
    hi = v.astype(BF16)
    lo = (v - hi.astype(F32)).astype(BF16)
    return hi, lo


def _dot(a, b):
    return jnp.dot(a, b, preferred_element_type=F32)


def _dot3(a_hi, a_lo, b_hi, b_lo):
    return _dot(a_hi, b_hi) + (_dot(a_lo, b_hi) + _dot(a_hi, b_lo))


def _store_token_major(ref, v, group=0, groups=1):
    t, width = v.shape
    c = width // LANES
    for s in range(c):
        ref[pl.ds(group * c + s, t, stride=groups * c), :] = v[:, s * LANES:(s + 1) * LANES]


def _load_token_major(ref, t, c, group=0, groups=1):
    return jnp.concatenate(
        [ref[pl.ds(group * c + s, t, stride=groups * c), :] for s in range(c)], axis=-1)


def _cumsum_rows(v, rows):
    t = v.shape[0]
    k = 1
    while k < t:
        v = v + jnp.where(rows >= k, pltpu.roll(v, k, 0), 0.0)
        k *= 2
    return v


def _ada_kernel(c_ref, w_ref, b_ref, o_ref):
    cond = _silu(c_ref[...])
    c_hi, c_lo = _split_bf16(cond)
    w_hi, w_lo = _split_bf16(w_ref[0])
    o_ref[0] = _dot3(c_hi, c_lo, w_hi, w_lo) + b_ref[0]


def _ada_call(c, w_ada, b_ada):
    depth, d, width = w_ada.shape
    bsz = c.shape[0]
    rows = 8
    c_pad = jnp.zeros((rows, d), F32).at[:bsz].set(c)
    tn = 1024
    out = pl.pallas_call(
        _ada_kernel,
        grid=(depth, width // tn),
        in_specs=[
            pl.BlockSpec((rows, d), lambda l, j: (0, 0)),
            pl.BlockSpec((1, d, tn), lambda l, j: (l, 0, j)),
            pl.BlockSpec((1, 1, tn), lambda l, j: (l, 0, j)),
        ],
        out_specs=pl.BlockSpec((1, rows, tn), lambda l, j: (l, 0, j)),
        out_shape=jax.ShapeDtypeStruct((depth, rows, width), F32),
        compiler_params=_cparams(("arbitrary", "arbitrary")),
        name="ada_mod",
    )(c_pad, w_ada, b_ada.reshape(depth, 1, width))
    return out[:, :bsz]


W_IN_TN = 1024
W_IN_SKIPS = ((3 * ATT_WIDTH // W_IN_TN, ATT_HEADS),
              (COL_GATE // W_IN_TN, ATT_HEADS + SSM_HEADS))


def _w_in_kernel(a_ref, b_ref, o_ref):
    jb = pl.program_id(1)
    (j1, s1), (j2, s2) = W_IN_SKIPS

    def emit(shift):
        if shift == 0:
            o_ref[...] = a_ref[...].astype(BF16)
        else:
            both = jnp.concatenate([a_ref[...], b_ref[...]], axis=1)
            o_ref[...] = both[:, shift:shift + W_IN_TN].astype(BF16)

    pl.when(jb < j1)(lambda: emit(0))
    pl.when((jb >= j1) & (jb < j2))(lambda: emit(s1))
    pl.when(jb >= j2)(lambda: emit(s2))


def _w_in_call(w_in):
    depth, d, _ = w_in.shape
    per = W_IN_TN // LANES
    return pl.pallas_call(
        _w_in_kernel,
        grid=(depth, BIG_W // W_IN_TN),
        in_specs=[
            pl.BlockSpec((None, d, W_IN_TN), lambda l, j: (l, 0, j)),
            pl.BlockSpec((None, d, LANES), lambda l, j: (l, 0, (j + 1) * per)),
        ],
        out_specs=pl.BlockSpec((None, d, W_IN_TN), lambda l, j: (l, 0, j)),
        out_shape=jax.ShapeDtypeStruct((depth, d, BIG_W), BF16),
        compiler_params=_cparams(("arbitrary", "arbitrary")),
        name="w_in_relayout",
    )(w_in, w_in)


def _inproj_kernel(x_ref, sh_ref, sc_ref, w_ref, wsh_ref, wsl_ref, big_ref, small_ref, h_ref):
    @pl.when(pl.program_id(1) == 0)
    def _():
        h = _layer_norm(x_ref[...]) * (1.0 + sc_ref[0]) + sh_ref[0]
        h_hi, h_lo = _split_bf16(h)
        h_ref[...] = h_hi
        small_ref[...] = _dot3(h_hi, h_lo, wsh_ref[...], wsl_ref[...])

    big_ref[...] = _dot(h_ref[...], w_ref[...]).astype(BF16)


def _inproj_call(x2, shift, scale, w_big, ws_hi, ws_lo, layer, seq):
    n, d = x2.shape
    tm, tn = min(1024, seq), 1024
    tiles_per_seq = seq // tm
    return pl.pallas_call(
        _inproj_kernel,
        grid=(n // tm, BIG_W // tn),
        in_specs=[
            pl.BlockSpec((tm, d), lambda i, j: (i, 0)),
            pl.BlockSpec((1, 1, d), lambda i, j: (i // tiles_per_seq, 0, 0)),
            pl.BlockSpec((1, 1, d), lambda i, j: (i // tiles_per_seq, 0, 0)),
            pl.BlockSpec((None, d, tn), lambda i, j: (layer, 0, j)),
            pl.BlockSpec((None, d, SMALL_W), lambda i, j: (layer, 0, 0)),
            pl.BlockSpec((None, d, SMALL_W), lambda i, j: (layer, 0, 0)),
        ],
        out_specs=[
            pl.BlockSpec((tm, tn), lambda i, j: (i, j)),
            pl.BlockSpec((tm, SMALL_W), lambda i, j: (i, 0)),
        ],
        out_shape=[
            jax.ShapeDtypeStruct((n, BIG_W), BF16),
            jax.ShapeDtypeStruct((n, SMALL_W), F32),
        ],
        scratch_shapes=[pltpu.VMEM((tm, d), BF16)],
        compiler_params=_cparams(("arbitrary", "arbitrary")),
        name="inproj",
    )(x2, shift, scale, w_big, ws_hi, ws_lo)


def _fcum_kernel(s_ref, bf_ref, o_ref, carry_ref):
    @pl.when(pl.program_id(1) == 0)
    def _():
        carry_ref[...] = jnp.zeros_like(carry_ref)

    lf = _log_sigmoid(s_ref[...] + bf_ref[...])
    rows = lax.broadcasted_iota(jnp.int32, lf.shape, 0)
    out = _cumsum_rows(lf, rows) + carry_ref[0:1, :]
    o_ref[...] = out
    carry_ref[0:1, :] = out[lf.shape[0] - 1:, :]


def _fcum_call(small, bf_row, bsz, seq):
    t = 512
    per = seq // t
    return pl.pallas_call(
        _fcum_kernel,
        grid=(bsz, per),
        in_specs=[
            pl.BlockSpec((t, SMALL_W), lambda b, i: (b * per + i, 0)),
            pl.BlockSpec((1, SMALL_W), lambda b, i: (0, 0)),
        ],
        out_specs=pl.BlockSpec((t, SMALL_W), lambda b, i: (b * per + i, 0)),
        out_shape=jax.ShapeDtypeStruct(small.shape, F32),
        scratch_shapes=[pltpu.VMEM((8, SMALL_W), F32)],
        compiler_params=_cparams(("arbitrary", "arbitrary")),
        name="forget_cumsum",
    )(small, bf_row)


def _split3_bf16(v):
    hi = v.astype(BF16)
    r = v - hi.astype(F32)
    mid = r.astype(BF16)
    lo = (r - mid.astype(F32)).astype(BF16)
    return hi, mid, lo


def _attn_kernel(q_ref, k_ref, v_ref, f_ref, o_ref, kx_ref, vx_ref, *, tq, tk):
    hp = pl.program_id(1)
    qi = pl.program_id(2)
    dh = ATT_HEAD_DIM
    seq = k_ref.shape[0]

    def head_lanes(rows, hh):
        lane = lax.broadcasted_iota(jnp.int32, (rows, LANES), 1)
        in_head = (lane >= hh * dh) & (lane < (hh + 1) * dh)
        return lane - (1 - hh) * dh, in_head

    def f_column(f_tile, hh):
        lane = lax.broadcasted_iota(jnp.int32, f_tile.shape, 1)
        return jnp.sum(jnp.where(lane == F_COL0 + 2 * hp + hh, f_tile, 0.0), axis=-1, keepdims=True)

    def f_terms(f_tile, hh):
        return [t.astype(F32) for t in _split3_bf16(f_column(f_tile, hh))]

    @pl.when(qi == 0)
    def _():
        f_all = f_ref[...]
        for hh in range(2):
            ext, in_head = head_lanes(seq, hh)
            hi, mid, lo = f_terms(f_all, hh)
            k_sp = jnp.where((ext >= 0) & (ext < 3), 1.0,
                             jnp.where(ext == 3, -hi, jnp.where(ext == 4, -mid,
                                                                jnp.where(ext == 5, -lo, 0.0))))
            kx_ref[hh] = jnp.where(in_head, k_ref[...], k_sp.astype(BF16))
            vx_ref[hh] = jnp.where(in_head, v_ref[...], jnp.where(ext == 0, 1.0, 0.0).astype(BF16))

    f_q = f_ref[pl.ds(pl.multiple_of(qi * tq, tq), tq), :]
    r_iota = lax.broadcasted_iota(jnp.int32, (tk, tk), 0)
    c_iota = lax.broadcasted_iota(jnp.int32, (tk, tk), 1)
    sub = tq // tk
    nt = (((1,), (1,)), ((), ()))
    qx = []
    for hh in range(2):
        ext, in_head = head_lanes(tq, hh)
        hi, mid, lo = f_terms(f_q, hh)
        q_sp = jnp.where(ext == 0, hi, jnp.where(ext == 1, mid, jnp.where(ext == 2, lo,
                         jnp.where((ext >= 3) & (ext < 6), 1.0, 0.0))))
        qx.append(jnp.where(in_head, q_ref[...] * 0.125, q_sp.astype(BF16)))

    def block(hh, j, state, diag):
        off = pl.multiple_of(j * tk, tk)
        first = 0 if diag is None else diag
        s = lax.dot_general(qx[hh][first * tk:], kx_ref[hh, pl.ds(off, tk), :], nt,
                            preferred_element_type=F32)
        new_m, probs = [], []
        for r in range(first, sub):
            s_r = s[(r - first) * tk:(r - first + 1) * tk]
            if diag is not None and r == diag:
                s_r = jnp.where(r_iota >= c_iota, s_r, -jnp.inf)
            s_max = jnp.max(s_r, axis=-1, keepdims=True)
            m_r = s_max if state[r] is None else jnp.maximum(state[r][0], s_max)
            new_m.append(m_r)
            probs.append(jnp.exp((s_r - m_r).astype(BF16)))
        pv = _dot(probs[0] if len(probs) == 1 else jnp.concatenate(probs, axis=0),
                  vx_ref[hh, pl.ds(off, tk), :])
        out = list(state)
        for r in range(first, sub):
            pv_r = pv[(r - first) * tk:(r - first + 1) * tk]
            m_r = new_m[r - first]
            out[r] = (m_r, pv_r if state[r] is None
                      else jnp.exp(state[r][0] - m_r) * state[r][1] + pv_r)
        return out

    st0 = st1 = [None] * sub
    for dg in range(sub):
        st0 = block(0, qi * sub + dg, st0, dg)
        st1 = block(1, qi * sub + dg, st1, dg)

    def body(j, carry):
        st0, st1 = carry
        return block(0, j, st0, None), block(1, j, st1, None)

    st0, st1 = lax.fori_loop(0, qi * sub, body, (st0, st1))
    a0 = jnp.concatenate([acc for _, acc in st0], axis=0) if sub > 1 else st0[0][1]
    a1 = jnp.concatenate([acc for _, acc in st1], axis=0) if sub > 1 else st1[0][1]
    out = None
    for hh, acc in ((0, a0), (1, a1)):
        ext, in_head = head_lanes(tq, hh)
        denom = jnp.sum(jnp.where(ext == 0, acc, 0.0), axis=-1, keepdims=True)
        o = acc / denom
        out = o if out is None else jnp.where(in_head, o, out)
    o_ref[...] = out.astype(BF16)


def _attn_call(big3, fcol):
    bsz, seq, _ = big3.shape
    tq = min(2048, seq)
    tk = min(512, tq)
    nq = seq // tq
    pairs = ATT_HEADS // 2
    kb, vb = COL_K // LANES, COL_V // LANES
    return pl.pallas_call(
        functools.partial(_attn_kernel, tq=tq, tk=tk),
        grid=(bsz, pairs, nq),
        in_specs=[
            pl.BlockSpec((None, tq, LANES), lambda b, h, i: (b, i, h)),
            pl.BlockSpec((None, seq, LANES), lambda b, h, i: (b, 0, kb + h)),
            pl.BlockSpec((None, seq, LANES), lambda b, h, i: (b, 0, vb + h)),
            pl.BlockSpec((seq, SMALL_W), lambda b, h, i: (b, 0)),
        ],
        out_specs=pl.BlockSpec((None, tq, LANES), lambda b, h, i: (b, i, h)),
        out_shape=jax.ShapeDtypeStruct((bsz, seq, ATT_WIDTH), BF16),
        scratch_shapes=[pltpu.VMEM((2, seq, LANES), BF16), pltpu.VMEM((2, seq, LANES), BF16)],
        compiler_params=_cparams(("arbitrary", "arbitrary", "arbitrary")),
        name="fox_attention",
    )(big3, big3, big3, fcol)


def _ssd_kernel(z_ref, xbc_ref, dt_ref, cw_ref, cb_ref, dtb_ref, alog_ref, dsk_ref, nw_ref,
                y_ref, ext_ref, state_ref):
    L = SSD_CHUNK
    P = SSM_HEAD_DIM
    GW = SSM_GROUP_WIDTH
    NS = SSM_STATE
    halo = 8

    @pl.when(pl.program_id(1) == 0)
    def _():
        ext_ref[0:halo, :] = jnp.zeros((halo, SSM_CONV_DIM), F32)
        state_ref[...] = jnp.zeros_like(state_ref)

    ext_ref[halo:halo + L, :] = xbc_ref[...].astype(F32)
    ext = ext_ref[...]
    conv = cb_ref[...] + cw_ref[SSM_CONV - 1:SSM_CONV, :] * ext[halo:, :]
    for lag in range(1, SSM_CONV):
        j = SSM_CONV - 1 - lag
        conv = conv + cw_ref[j:j + 1, :] * pltpu.roll(ext, lag, 0)[halo:, :]
    ext_ref[0:halo, :] = ext[L:L + halo, :]
    xc = _silu(conv)

    rows = lax.broadcasted_iota(jnp.int32, (L, LANES), 0)
    lane = lax.broadcasted_iota(jnp.int32, (L, LANES), 1)
    lo_half = lane < P
    dt = _softplus(dt_ref[...] + dtb_ref[...])
    a_cs = _cumsum_rows(dt * (-jnp.exp(alog_ref[...])), rows)
    a_t = jnp.concatenate([a_cs, a_cs], axis=0).T
    causal = rows >= jnp.where(lo_half, lane, lane - P)
    blk = (lax.broadcasted_iota(jnp.int32, (2 * L, LANES), 0) < L) == \
          (lax.broadcasted_iota(jnp.int32, (2 * L, LANES), 1) < P)

    def pair_cols(mat, h0):
        r = mat.shape[0]
        c0 = jnp.broadcast_to(mat[:, h0:h0 + 1], (r, LANES))
        c1 = jnp.broadcast_to(mat[:, h0 + 1:h0 + 2], (r, LANES))
        return jnp.where(lo_half[:r], c0, c1)

    def pair_rows(mat_t, h0):
        return jnp.where(lo_half[:1], mat_t[h0:h0 + 1, :], mat_t[h0 + 1:h0 + 2, :])

    for g in range(SSM_GROUPS):
        bm = xc[:, SSM_INNER + g * NS:SSM_INNER + (g + 1) * NS]
        cm = xc[:, SSM_INNER + SSM_GROUPS * NS + g * NS:SSM_INNER + SSM_GROUPS * NS + (g + 1) * NS]
        cm_b = cm.astype(BF16)
        bm_t2 = jnp.concatenate([bm, bm], axis=0).T.astype(BF16)
        cb2 = _dot(cm_b, bm_t2)
        st_prev = state_ref[g]
        y_off = _dot(cm_b, st_prev.astype(BF16))
        y_parts, xw_parts, cd_parts = [], [], []
        for pr in range(GW // LANES):
            h0 = g * (GW // P) + 2 * pr
            c0 = g * GW + pr * LANES
            xs_p = xc[:, c0:c0 + LANES]
            a_col = pair_cols(a_cs, h0)
            a_end = a_col[L - 1:L, :]
            seg = a_col - pair_rows(a_t, h0)
            decay = jnp.where(causal, jnp.exp(jnp.where(causal, seg, 0.0)), 0.0)
            xdt = xs_p * pair_cols(dt, h0)
            xdt2 = jnp.concatenate([xdt, xdt], axis=0)
            xdt_bd = jnp.where(blk, xdt2, 0.0).astype(BF16)
            y_d = _dot((cb2 * decay).astype(BF16), xdt_bd)
            y_o = y_off[:, pr * LANES:(pr + 1) * LANES] * jnp.exp(a_col)
            y_parts.append(y_d + y_o + dsk_ref[:, c0:c0 + LANES] * xs_p)
            xw_parts.append(xdt * jnp.exp(a_end - a_col))
            cd_parts.append(jnp.exp(a_end))
        xw = jnp.concatenate(xw_parts, axis=-1).astype(BF16)
        state_ref[g] = st_prev * jnp.concatenate(cd_parts, axis=-1) + _dot(bm_t2[:, :L], xw)
        y = jnp.concatenate(y_parts, axis=-1) * _silu(z_ref[:, g * GW:(g + 1) * GW].astype(F32))
        y = y * lax.rsqrt(jnp.mean(y * y, axis=-1, keepdims=True) + RMS_EPS)
        y_ref[:, g * GW:(g + 1) * GW] = (y * nw_ref[:, g * GW:(g + 1) * GW]).astype(BF16)


def _ssd_call(big, small, conv_w, conv_b, dtb_row, alog_row, dsk_row, nw_row, bsz, seq):
    n = big.shape[0]
    L = SSD_CHUNK
    nc = seq // L
    const = lambda b, c: (0, 0)
    return pl.pallas_call(
        _ssd_kernel,
        grid=(bsz, nc),
        in_specs=[
            pl.BlockSpec((L, SSM_INNER), lambda b, c: (b * nc + c, COL_Z // SSM_INNER)),
            pl.BlockSpec((L, SSM_CONV_DIM), lambda b, c: (b * nc + c, COL_XBC // SSM_CONV_DIM)),
            pl.BlockSpec((L, SMALL_W), lambda b, c: (b * nc + c, 0)),
            pl.BlockSpec((SSM_CONV, SSM_CONV_DIM), const),
            pl.BlockSpec((1, SSM_CONV_DIM), const),
            pl.BlockSpec((1, SMALL_W), const),
            pl.BlockSpec((1, SMALL_W), const),
            pl.BlockSpec((1, SSM_INNER), const),
            pl.BlockSpec((1, SSM_INNER), const),
        ],
        out_specs=pl.BlockSpec((L, SSM_INNER), lambda b, c: (b * nc + c, 0)),
        out_shape=jax.ShapeDtypeStruct((n, SSM_INNER), BF16),
        scratch_shapes=[
            pltpu.VMEM((8 + L, SSM_CONV_DIM), F32),
            pltpu.VMEM((SSM_GROUPS, SSM_STATE, SSM_GROUP_WIDTH), F32),
        ],
        compiler_params=_cparams(("arbitrary", "arbitrary")),
        name="ssd_scan",
    )(big, big, small, conv_w, conv_b, dtb_row, alog_row, dsk_row, nw_row)


def _mixout_kernel(x_ref, ya_ref, ys_ref, u_ref, uh_ref, gate_ref, gm_ref, shf_ref, scf_ref,
                   wao_ref, wso_ref, wp_ref, ps_ref, wout_ref, lng_ref, lnb_ref, wrh_ref, wrl_ref,
                   x1_ref, h2_ref, lt_ref, ext_ref, *, tm, tiles_per_seq):
    i = pl.program_id(0)
    halo = 16
    first = (i % tiles_per_seq) == 0
    ext_ref[0:halo, :] = jnp.where(first, 0.0, uh_ref[...].astype(F32))
    u = u_ref[...].astype(F32)
    ext_ref[halo:halo + tm, :] = u
    pos = ((i % tiles_per_seq) * tm + 1 + lax.broadcasted_iota(jnp.int32, (tm, 1), 0)).astype(F32)

    y_att = _dot(ya_ref[...], wao_ref[...])
    y_ssm = _dot(ys_ref[...], wso_ref[...])
    pool_parts = []
    for g, w in enumerate(POOL_WINDOWS):
        c0 = g * POOL_GROUP_DIM
        s = ext_ref[:, c0:c0 + POOL_GROUP_DIM]
        k = 1
        while k < w:
            s = s + pltpu.roll(s, k, 0)
            k *= 2
        pooled = s[halo:, :] / jnp.minimum(pos, float(w)) - u[:, c0:c0 + POOL_GROUP_DIM]
        pool_parts.append(_dot(pooled.astype(BF16), wp_ref[g]))
    y_pool = jnp.concatenate(pool_parts, axis=-1) * ps_ref[...]

    g_att = _sigmoid(gate_ref[:, 0:D_MODEL].astype(F32))
    g_pool = _sigmoid(gate_ref[:, D_MODEL:2 * D_MODEL].astype(F32))
    g_ssm = _sigmoid(gate_ref[:, 2 * D_MODEL:3 * D_MODEL].astype(F32))
    merged = g_att * y_att + g_pool * y_pool + g_ssm * y_ssm
    mix = _dot(merged.astype(BF16), wout_ref[...])
    x1 = _layer_norm(DEEPNORM_ALPHA * x_ref[...] + gm_ref[0] * mix) * lng_ref[...] + lnb_ref[...]
    x1_ref[...] = x1
    h2 = _layer_norm(x1) * (1.0 + scf_ref[0]) + shf_ref[0]
    _store_token_major(h2_ref, h2)
    h_hi, h_lo = _split_bf16(h2)
    nt = (((1,), (1,)), ((), ()))
    lt_ref[...] = (lax.dot_general(wrh_ref[...], h_hi, nt, preferred_element_type=F32)
                   + (lax.dot_general(wrh_ref[...], h_lo, nt, preferred_element_type=F32)
                      + lax.dot_general(wrl_ref[...], h_hi, nt, preferred_element_type=F32)))


def _mixout_call(x2, y_att, y_ssm, big, gm, shf, scf, wao, wso, wp, ps_row, wout, lng, lnb,
                 wr_hi, wr_lo, seq):
    n, d = x2.shape
    tm = 512
    halo = 16
    tiles_per_seq = seq // tm
    const2 = lambda i: (0, 0)
    per_seq = lambda i: (i // tiles_per_seq, 0, 0)
    return pl.pallas_call(
        functools.partial(_mixout_kernel, tm=tm, tiles_per_seq=tiles_per_seq),
        grid=(n // tm,),
        in_specs=[
            pl.BlockSpec((tm, d), lambda i: (i, 0)),
            pl.BlockSpec((tm, ATT_WIDTH), lambda i: (i, 0)),
            pl.BlockSpec((tm, SSM_INNER), lambda i: (i, 0)),
            pl.BlockSpec((tm, POOL_WIDTH), lambda i: (i, COL_POOL // POOL_WIDTH)),
            pl.BlockSpec((halo, POOL_WIDTH),
                         lambda i: (jnp.maximum(i * (tm // halo) - 1, 0), COL_POOL // POOL_WIDTH)),
            pl.BlockSpec((tm, N_BRANCHES * d), lambda i: (i, COL_GATE // (N_BRANCHES * d))),
            pl.BlockSpec((1, 1, d), per_seq),
            pl.BlockSpec((1, 1, d), per_seq),
            pl.BlockSpec((1, 1, d), per_seq),
            pl.BlockSpec((ATT_WIDTH, d), const2),
            pl.BlockSpec((SSM_INNER, d), const2),
            pl.BlockSpec((POOL_GROUPS, POOL_GROUP_DIM, POOL_GROUP_DIM), lambda i: (0, 0, 0)),
            pl.BlockSpec((1, d), const2),
            pl.BlockSpec((d, d), const2),
            pl.BlockSpec((1, d), const2),
            pl.BlockSpec((1, d), const2),
            pl.BlockSpec((N_EXPERTS, d), const2),
            pl.BlockSpec((N_EXPERTS, d), const2),
        ],
        out_specs=[
            pl.BlockSpec((tm, d), lambda i: (i, 0)),
            pl.BlockSpec((tm * (d // LANES), LANES), lambda i: (i, 0)),
            pl.BlockSpec((N_EXPERTS, tm), lambda i: (0, i)),
        ],
        out_shape=[
            jax.ShapeDtypeStruct((n, d), F32),
            jax.ShapeDtypeStruct((n * (d // LANES), LANES), F32),
            jax.ShapeDtypeStruct((N_EXPERTS, n), F32),
        ],
        scratch_shapes=[pltpu.VMEM((halo + tm, POOL_WIDTH), F32)],
        compiler_params=_cparams(("arbitrary",)),
        name="mixer_out",
    )(x2, y_att, y_ssm, big, big, big, gm, shf, scf, wao, wso, wp, ps_row, wout, lng, lnb,
      wr_hi, wr_lo)


def _top2(vals):
    n = len(vals)
    v1 = vals[0]
    for v in vals[1:]:
        v1 = jnp.maximum(v1, v)
    i1 = jnp.full(v1.shape, n, jnp.int32)
    for j in reversed(range(n)):
        i1 = jnp.where(vals[j] == v1, j, i1)
    v2 = jnp.full(v1.shape, -jnp.inf, F32)
    for j in range(n):
        v2 = jnp.maximum(v2, jnp.where(i1 == j, -jnp.inf, vals[j]))
    i2 = jnp.full(v1.shape, n, jnp.int32)
    for j in reversed(range(n)):
        i2 = jnp.where((vals[j] == v2) & (i1 != j), j, i2)
    return v1, i1, v2, i2


def _router_kernel(lt_ref, br_ref, e_ref, w_ref, cnt_ref):
    lg = lt_ref[...]
    m = jnp.max(lg, axis=0, keepdims=True)
    ex = jnp.exp(lg - m)
    probs = ex / jnp.sum(ex, axis=0, keepdims=True)
    sel = probs + br_ref[...]
    tops = []
    for g in range(N_EXPERT_GROUPS):
        vals = [sel[g * EXPERTS_PER_GROUP + j:g * EXPERTS_PER_GROUP + j + 1, :]
                for j in range(EXPERTS_PER_GROUP)]
        tops.append(_top2(vals))
    best = tops[0][0] + tops[0][2]
    e1 = tops[0][1]
    e2 = tops[0][3]
    for g in range(1, N_EXPERT_GROUPS):
        score = tops[g][0] + tops[g][2]
        better = score > best
        best = jnp.where(better, score, best)
        e1 = jnp.where(better, tops[g][1] + g * EXPERTS_PER_GROUP, e1)
        e2 = jnp.where(better, tops[g][3] + g * EXPERTS_PER_GROUP, e2)
    p1 = jnp.zeros_like(best)
    p2 = jnp.zeros_like(best)
    for e in range(N_EXPERTS):
        p1 = jnp.where(e1 == e, probs[e:e + 1, :], p1)
        p2 = jnp.where(e2 == e, probs[e:e + 1, :], p2)
    tot = p1 + p2
    e_ref[...] = jnp.concatenate([e1, e2], axis=0)
    w_ref[...] = jnp.concatenate([p1 / tot, p2 / tot], axis=0)

    @pl.when(pl.program_id(0) == 0)
    def _():
        cnt_ref[...] = jnp.zeros_like(cnt_ref)

    e_rows = lax.broadcasted_iota(jnp.int32, lg.shape, 0)
    hits = ((e_rows == e1) | (e_rows == e2)).astype(jnp.int32)
    cnt_ref[...] += jnp.sum(hits, axis=1, keepdims=True)


def _router_call(logits_t, br_col):
    e, n = logits_t.shape
    t = min(2048, n)
    return pl.pallas_call(
        _router_kernel,
        grid=(n // t,),
        in_specs=[
            pl.BlockSpec((e, t), lambda i: (0, i)),
            pl.BlockSpec((e, 1), lambda i: (0, 0)),
        ],
        out_specs=[
            pl.BlockSpec((TOP_K, t), lambda i: (0, i)),
            pl.BlockSpec((TOP_K, t), lambda i: (0, i)),
            pl.BlockSpec((e, LANES), lambda i: (0, 0)),
        ],
        out_shape=[
            jax.ShapeDtypeStruct((TOP_K, n), jnp.int32),
            jax.ShapeDtypeStruct((TOP_K, n), F32),
            jax.ShapeDtypeStruct((e, LANES), jnp.int32),
        ],
        compiler_params=_cparams(("arbitrary",)),
        name="router_top2",
    )(logits_t, br_col)


def _expert_kernel(tok_ref, dst_ref, bexp_ref, h_hbm,
                   w1a_ref, w3a_ref, w2a_ref, swa_ref, w1b_ref, w3b_ref, w2b_ref, swb_ref,
                   y_hbm, xa, xb, ya, yb, w13a_bf, w2a_bf, w13b_bf, w2b_bf, gsem, ssem):
    i = pl.program_id(0)
    last = pl.num_programs(0) - 1
    c = w1a_ref.shape[1] // LANES

    def start_gather(blk, xbuf, sem):
        base = blk * MOE_BLOCK
        for k in range(MOE_BLOCK):
            row = pl.multiple_of(tok_ref[base + k], c)
            pltpu.make_async_copy(h_hbm.at[pl.ds(row, c), :], xbuf.at[pl.ds(k * c, c), :],
                                  sem).start(priority=k % DMA_PRIORITIES)

    def start_scatter(blk, ybuf, sem):
        base = (blk + 1) * MOE_BLOCK
        for k in range(MOE_BLOCK):
            row = pl.multiple_of(dst_ref[base + k], c)
            pltpu.make_async_copy(ybuf.at[pl.ds(k * c, c), :], y_hbm.at[pl.ds(row, c), :],
                                  sem).start(priority=k % DMA_PRIORITIES)

    def wait_gather(xbuf, sem):
        pltpu.make_async_copy(h_hbm.at[pl.ds(0, MOE_BLOCK * c), :], xbuf, sem).wait()

    def wait_scatter(ybuf, sem):
        pltpu.make_async_copy(ybuf, y_hbm.at[pl.ds(0, MOE_BLOCK * c), :], sem).wait()

    def ffn(xbuf, w_bf, sw_ref, ybuf):
        w13_bf, w2_bf = w_bf
        xv = _load_token_major(xbuf, MOE_BLOCK, c).astype(BF16)
        hid = _silu(_dot(xv, w13_bf[0])) * _dot(xv, w13_bf[1])
        _store_token_major(ybuf, _dot(hid.astype(BF16), w2_bf[...]) * sw_ref[...])

    def refresh_weights(parity, w1_ref, w3_ref, w2_ref, w_bf):
        w13_bf, w2_bf = w_bf
        blk = 2 * i + parity
        changed = (i == 0) | (bexp_ref[blk] != bexp_ref[jnp.maximum(blk - 2, 0)])

        @pl.when(changed)
        def _():
            w13_bf[0] = w1_ref[0].astype(BF16)
            w13_bf[1] = w3_ref[0].astype(BF16)
            w2_bf[...] = w2_ref[0].astype(BF16)

    @pl.when(i == 0)
    def _():
        yb[...] = jnp.zeros_like(yb)
        start_gather(0, xa, gsem.at[0])

    @pl.when(i > 0)
    def _():
        wait_scatter(ya, ssem.at[0])

    refresh_weights(0, w1a_ref, w3a_ref, w2a_ref, (w13a_bf, w2a_bf))
    refresh_weights(1, w1b_ref, w3b_ref, w2b_ref, (w13b_bf, w2b_bf))

    wait_gather(xa, gsem.at[0])
    start_gather(2 * i + 1, xb, gsem.at[1])
    start_scatter(2 * i - 1, yb, ssem.at[1])
    ffn(xa, (w13a_bf, w2a_bf), swa_ref, ya)

    wait_gather(xb, gsem.at[1])
    wait_scatter(yb, ssem.at[1])
    start_gather(2 * i + 2, xa, gsem.at[0])
    start_scatter(2 * i, ya, ssem.at[0])
    ffn(xb, (w13b_bf, w2b_bf), swb_ref, yb)

    @pl.when(i == last)
    def _():
        wait_gather(xa, gsem.at[0])
        start_scatter(2 * i + 1, yb, ssem.at[1])
        wait_scatter(ya, ssem.at[0])
        wait_scatter(yb, ssem.at[1])


def _expert_call(tok_ext, dst_ext, blk_expert, h2_tiles, w1, w3, w2, layer, slot_w):
    d = w1.shape[2]
    c = d // LANES
    p = slot_w.shape[0]
    nblk = p // MOE_BLOCK
    assert nblk % 2 == 0 and tok_ext.shape[0] == p + MOE_BLOCK and dst_ext.shape[0] == p + MOE_BLOCK

    def weight_spec(shape, parity):
        return pl.BlockSpec(shape, lambda i, tok, dst, be: (layer, be[2 * i + parity], 0, 0))

    def block_specs(parity):
        return [weight_spec((None, 1, d, EXPERT_DFF), parity), weight_spec((None, 1, d, EXPERT_DFF), parity),
                weight_spec((None, 1, EXPERT_DFF, d), parity),
                pl.BlockSpec((MOE_BLOCK, 1), lambda i, tok, dst, be: (2 * i + parity, 0))]

    grid_spec = pltpu.PrefetchScalarGridSpec(
        num_scalar_prefetch=3,
        grid=(nblk // 2,),
        in_specs=[pl.BlockSpec(memory_space=pl.ANY)] + block_specs(0) + block_specs(1),
        out_specs=pl.BlockSpec(memory_space=pl.ANY),
        scratch_shapes=[pltpu.VMEM((MOE_BLOCK * c, LANES), F32)] * 4
        + [pltpu.VMEM((2, d, EXPERT_DFF), BF16), pltpu.VMEM((EXPERT_DFF, d), BF16)] * 2
        + [pltpu.SemaphoreType.DMA((2,)), pltpu.SemaphoreType.DMA((2,))],
    )
    return pl.pallas_call(
        _expert_kernel,
        grid_spec=grid_spec,
        out_shape=jax.ShapeDtypeStruct(((p + MOE_BLOCK) * c, LANES), F32),
        compiler_params=_cparams(("arbitrary",)),
        name="moe_experts",
    )(tok_ext, dst_ext, blk_expert, h2_tiles, w1, w3, w2, slot_w, w1, w3, w2, slot_w)


def _combine_kernel(x_ref, y_ref, gf_ref, lng_ref, lnb_ref, o_ref):
    tm, d = x_ref.shape
    c = d // LANES
    ffn = _load_token_major(y_ref, tm, c, 0, TOP_K) + _load_token_major(y_ref, tm, c, 1, TOP_K)
    o_ref[...] = _layer_norm(DEEPNORM_ALPHA * x_ref[...] + gf_ref[0] * ffn) * lng_ref[...] + lnb_ref[...]


def _combine_call(x1, y_pairs, gf, lng, lnb, seq):
    n, d = x1.shape
    tm = 512
    tiles_per_seq = seq // tm
    return pl.pallas_call(
        _combine_kernel,
        grid=(n // tm,),
        in_specs=[
            pl.BlockSpec((tm, d), lambda i: (i, 0)),
            pl.BlockSpec((tm * TOP_K * (d // LANES), LANES), lambda i: (i, 0)),
            pl.BlockSpec((1, 1, d), lambda i: (i // tiles_per_seq, 0, 0)),
            pl.BlockSpec((1, d), lambda i: (0, 0)),
            pl.BlockSpec((1, d), lambda i: (0, 0)),
        ],
        out_specs=pl.BlockSpec((tm, d), lambda i: (i, 0)),
        out_shape=jax.ShapeDtypeStruct((n, d), F32),
        compiler_params=_cparams(("arbitrary",)),
        name="moe_combine",
    )(x1, y_pairs, gf, lng, lnb)


def _slot_tables(e_idx, gate_w, counts, n, tile_rows):
    nk = n * TOP_K
    e_flat = e_idx.reshape(nk)
    w_flat = gate_w.reshape(nk)
    order = jnp.argsort(e_flat).astype(jnp.int32)
    starts = jnp.cumsum(counts) - counts
    padded = ((counts + MOE_BLOCK - 1) // MOE_BLOCK) * MOE_BLOCK
    pends = jnp.cumsum(padded)
    pstarts = pends - padded
    p = nk + N_EXPERTS * MOE_BLOCK
    nblk = p // MOE_BLOCK
    blk_start = jnp.arange(nblk, dtype=jnp.int32) * MOE_BLOCK
    blk_expert = jnp.minimum(jnp.sum(pends[None, :] <= blk_start[:, None], axis=1),
                             N_EXPERTS - 1).astype(jnp.int32)
    per_slot = lambda tbl: jnp.repeat(tbl[blk_expert], MOE_BLOCK)
    rank = jnp.arange(p, dtype=jnp.int32) - per_slot(pstarts)
    is_pad = rank >= per_slot(counts)
    slot_flat = order[jnp.minimum(per_slot(starts) + rank, nk - 1)]
    slot_w = jnp.where(is_pad, 0.0, w_flat[slot_flat])
    pads_before = per_slot(pstarts - starts) + rank - per_slot(counts)
    slot_tok = jnp.where(is_pad, 0, slot_flat // TOP_K)
    slot_dst = jnp.where(is_pad, nk + pads_before, slot_flat)
    tok_ext = jnp.concatenate([slot_tok, jnp.zeros((MOE_BLOCK,), jnp.int32)])
    dst_ext = jnp.concatenate([p + jnp.arange(MOE_BLOCK, dtype=jnp.int32), slot_dst])
    return tok_ext * tile_rows, dst_ext * tile_rows, blk_expert, slot_w.reshape(p, 1)


def kernel(x, c, w_in, b_forget, w_attn_o, w_pool, pool_scale, conv_w, conv_b, dt_bias, a_log,
           d_skip, ssm_norm_w, w_ssm_o, w_out, w_ada, b_ada, ln_mix_g, ln_mix_b, ln_ffn_g,
           ln_ffn_b, w_router, b_router, w_exp_gate, w_exp_up, w_exp_down):
    bsz, seq, d = x.shape
    n = bsz * seq
    mod = _ada_call(c, w_ada, b_ada)
    wr_hi, wr_lo = _split_bf16(w_router.T)
    br_col = b_router.reshape(N_EXPERTS, 1)
    x2 = x.reshape(n, d)
    w_big = _w_in_call(w_in)
    o_f = 3 * ATT_WIDTH
    o_dt = o_f + ATT_HEADS + POOL_WIDTH + SSM_INNER + SSM_CONV_DIM
    w_small = jnp.concatenate(
        [w_in[:, :, o_dt:o_dt + SSM_HEADS], w_in[:, :, o_f:o_f + ATT_HEADS],
         jnp.zeros((DEPTH, d, SMALL_W - SSM_HEADS - ATT_HEADS), F32)], axis=2)
    ws_hi, ws_lo = _split_bf16(w_small)
    for l in range(DEPTH):
        sh_m, sc_m, g_m, sh_f, sc_f, g_f = (mod[l, :, i * d:(i + 1) * d].reshape(bsz, 1, d)
                                            for i in range(6))
        big, small = _inproj_call(x2, sh_m, sc_m, w_big, ws_hi, ws_lo, l, seq)

        bf_row = jnp.zeros((1, SMALL_W), F32).at[0, F_COL0:F_COL0 + ATT_HEADS].set(b_forget[l])
        fcol = _fcum_call(small, bf_row, bsz, seq)
        y_att = _attn_call(big.reshape(bsz, seq, BIG_W), fcol).reshape(n, ATT_WIDTH)

        pad_heads = jnp.zeros((SMALL_W - SSM_HEADS,), F32)
        dtb_row = jnp.concatenate([dt_bias[l], pad_heads]).reshape(1, SMALL_W)
        alog_row = jnp.concatenate([a_log[l], pad_heads]).reshape(1, SMALL_W)
        dsk_row = jnp.repeat(d_skip[l], SSM_HEAD_DIM).reshape(1, SSM_INNER)
        y_ssm = _ssd_call(big, small, conv_w[l], conv_b[l].reshape(1, SSM_CONV_DIM), dtb_row,
                          alog_row, dsk_row, ssm_norm_w[l].reshape(1, SSM_INNER), bsz, seq)

        x1, h2, logits_t = _mixout_call(
            x2, y_att, y_ssm, big, g_m, sh_f, sc_f, w_attn_o[l].astype(BF16),
            w_ssm_o[l].astype(BF16), w_pool[l].astype(BF16), pool_scale[l].reshape(1, d),
            w_out[l].astype(BF16), ln_mix_g[l].reshape(1, d), ln_mix_b[l].reshape(1, d),
            wr_hi, wr_lo, seq)

        e_idx_t, gate_t, counts = _router_call(logits_t, br_col)
        tok_ext, dst_ext, blk_expert, slot_w = _slot_tables(e_idx_t.T, gate_t.T, counts[:, 0], n,
                                                            d // LANES)
        y_pairs = _expert_call(tok_ext, dst_ext, blk_expert, h2,
                               w_exp_gate, w_exp_up, w_exp_down, l, slot_w)
        x2 = _combine_call(x1, y_pairs, g_f, ln_ffn_g[l].reshape(1, d), ln_ffn_b[l].reshape(1, d), seq)
    return x2.reshape(bsz, seq, d)
```

```python
import functools

import jax
import jax.numpy as jnp
from jax import lax
from jax.experimental import pallas as pl
from jax.experimental.pallas import tpu as pltpu

F32 = jnp.float32
BF16 = jnp.bfloat16

D_MODEL = 1024
DEPTH = 2
ATT_HEADS = 16
ATT_HEAD_DIM = 64
ATT_WIDTH = ATT_HEADS * ATT_HEAD_DIM
POOL_WINDOWS = (2, 4, 8, 16)
POOL_GROUPS = len(POOL_WINDOWS)
POOL_WIDTH = D_MODEL
POOL_GROUP_DIM = POOL_WIDTH // POOL_GROUPS
SSM_INNER = 2 * D_MODEL
SSM_HEAD_DIM = 64
SSM_HEADS = SSM_INNER // SSM_HEAD_DIM
SSM_GROUPS = 4
SSM_STATE = 128
SSM_CONV = 4
SSM_CONV_DIM = SSM_INNER + 2 * SSM_GROUPS * SSM_STATE
SSM_GROUP_WIDTH = SSM_INNER // SSM_GROUPS
N_BRANCHES = 3
N_EXPERTS = 16
N_EXPERT_GROUPS = 4
EXPERTS_PER_GROUP = N_EXPERTS // N_EXPERT_GROUPS
TOP_K = 2
EXPERT_DFF = 512
MOE_BLOCK = 256
DEEPNORM_ALPHA = (2 * DEPTH) ** 0.25
LN_EPS = 1e-5
RMS_EPS = 1e-6

LANES = 128
SSD_CHUNK = 64
DMA_PRIORITIES = 2
SMALL_W = LANES
F_COL0 = SSM_HEADS
COL_Q, COL_K, COL_V = 0, ATT_WIDTH, 2 * ATT_WIDTH
COL_POOL = 3 * ATT_WIDTH
COL_Z = COL_POOL + POOL_WIDTH
COL_XBC = COL_Z + SSM_INNER
COL_GATE = COL_XBC + SSM_CONV_DIM
BIG_W = COL_GATE + N_BRANCHES * D_MODEL
VMEM_LIMIT = 48 * 1024 * 1024


def _cparams(sem):
    return pltpu.CompilerParams(dimension_semantics=sem, vmem_limit_bytes=VMEM_LIMIT)


def _sigmoid(v):
    return 0.5 + 0.5 * jnp.tanh(0.5 * v)


def _silu(v):
    return v * _sigmoid(v)


def _softplus(v):
    return jnp.maximum(v, 0.0) + jnp.log1p(jnp.exp(-jnp.abs(v)))


def _log_sigmoid(v):
    return jnp.minimum(v, 0.0) - jnp.log1p(jnp.exp(-jnp.abs(v)))


def _layer_norm(v):
    mu = jnp.mean(v, axis=-1, keepdims=True)
    vc = v - mu
    return vc * lax.rsqrt(jnp.mean(vc * vc, axis=-1, keepdims=True) + LN_EPS)


def _split_bf16(v):
    hi = v.astype(BF16)
    lo = (v - hi.astype(F32)).astype(BF16)
    return hi, lo


def _dot(a, b):
    return jnp.dot(a, b, preferred_element_type=F32)


def _dot3(a_hi, a_lo, b_hi, b_lo):
    return _dot(a_hi, b_hi) + (_dot(a_lo, b_hi) + _dot(a_hi, b_lo))


def _store_token_major(ref, v, group=0, groups=1):
    t, width = v.shape
    c = width // LANES
    for s in range(c):
        ref[pl.ds(group * c + s, t, stride=groups * c), :] = v[:, s * LANES:(s + 1) * LANES]


def _load_token_major(ref, t, c, group=0, groups=1):
    return jnp.concatenate(
        [ref[pl.ds(group * c + s, t, stride=groups * c), :] for s in range(c)], axis=-1)


def _cumsum_rows(v, rows):
    t = v.shape[0]
    k = 1
    while k < t:
        v = v + jnp.where(rows >= k, pltpu.roll(v, k, 0), 0.0)
        k *= 2
    return v


def _ada_kernel(c_ref, w_ref, b_ref, o_ref):
    cond = _silu(c_ref[...])
    c_hi, c_lo = _split_bf16(cond)
    w_hi, w_lo = _split_bf16(w_ref[0])
    o_ref[0] = _dot3(c_hi, c_lo, w_hi, w_lo) + b_ref[0]


def _ada_call(c, w_ada, b_ada):
    depth, d, width = w_ada.shape
    bsz = c.shape[0]
    rows = 8
    c_pad = jnp.zeros((rows, d), F32).at[:bsz].set(c)
    tn = 1024
    out = pl.pallas_call(
        _ada_kernel,
        grid=(depth, width // tn),
        in_specs=[
            pl.BlockSpec((rows, d), lambda l, j: (0, 0)),
            pl.BlockSpec((1, d, tn), lambda l, j: (l, 0, j)),
            pl.BlockSpec((1, 1, tn), lambda l, j: (l, 0, j)),
        ],
        out_specs=pl.BlockSpec((1, rows, tn), lambda l, j: (l, 0, j)),
        out_shape=jax.ShapeDtypeStruct((depth, rows, width), F32),
        compiler_params=_cparams(("arbitrary", "arbitrary")),
        name="ada_mod",
    )(c_pad, w_ada, b_ada.reshape(depth, 1, width))
    return out[:, :bsz]


W_IN_TN = 1024
W_IN_SKIPS = ((3 * ATT_WIDTH // W_IN_TN, ATT_HEADS),
              (COL_GATE // W_IN_TN, ATT_HEADS + SSM_HEADS))


def _w_in_kernel(a_ref, b_ref, o_ref):
    jb = pl.program_id(1)
    (j1, s1), (j2, s2) = W_IN_SKIPS

    def emit(shift):
        if shift == 0:
            o_ref[...] = a_ref[...].astype(BF16)
        else:
            both = jnp.concatenate([a_ref[...], b_ref[...]], axis=1)
            o_ref[...] = both[:, shift:shift + W_IN_TN].astype(BF16)

    pl.when(jb < j1)(lambda: emit(0))
    pl.when((jb >= j1) & (jb < j2))(lambda: emit(s1))
    pl.when(jb >= j2)(lambda: emit(s2))


def _w_in_call(w_in):
    depth, d, _ = w_in.shape
    per = W_IN_TN // LANES
    return pl.pallas_call(
        _w_in_kernel,
        grid=(depth, BIG_W // W_IN_TN),
        in_specs=[
            pl.BlockSpec((None, d, W_IN_TN), lambda l, j: (l, 0, j)),
            pl.BlockSpec((None, d, LANES), lambda l, j: (l, 0, (j + 1) * per)),
        ],
        out_specs=pl.BlockSpec((None, d, W_IN_TN), lambda l, j: (l, 0, j)),
        out_shape=jax.ShapeDtypeStruct((depth, d, BIG_W), BF16),
        compiler_params=_cparams(("arbitrary", "arbitrary")),
        name="w_in_relayout",
    )(w_in, w_in)


def _inproj_kernel(x_ref, sh_ref, sc_ref, w_ref, wsh_ref, wsl_ref, big_ref, small_ref, h_ref):
    @pl.when(pl.program_id(1) == 0)
    def _():
        h = _layer_norm(x_ref[...]) * (1.0 + sc_ref[0]) + sh_ref[0]
        h_hi, h_lo = _split_bf16(h)
        h_ref[...] = h_hi
        small_ref[...] = _dot3(h_hi, h_lo, wsh_ref[...], wsl_ref[...])

    big_ref[...] = _dot(h_ref[...], w_ref[...]).astype(BF16)


def _inproj_call(x2, shift, scale, w_big, ws_hi, ws_lo, layer, seq):
    n, d = x2.shape
    tm, tn = min(1024, seq), 1024
    tiles_per_seq = seq // tm
    return pl.pallas_call(
        _inproj_kernel,
        grid=(n // tm, BIG_W // tn),
        in_specs=[
            pl.BlockSpec((tm, d), lambda i, j: (i, 0)),
            pl.BlockSpec((1, 1, d), lambda i, j: (i // tiles_per_seq, 0, 0)),
            pl.BlockSpec((1, 1, d), lambda i, j: (i // tiles_per_seq, 0, 0)),
            pl.BlockSpec((None, d, tn), lambda i, j: (layer, 0, j)),
            pl.BlockSpec((None, d, SMALL_W), lambda i, j: (layer, 0, 0)),
            pl.BlockSpec((None, d, SMALL_W), lambda i, j: (layer, 0, 0)),
        ],
        out_specs=[
            pl.BlockSpec((tm, tn), lambda i, j: (i, j)),
            pl.BlockSpec((tm, SMALL_W), lambda i, j: (i, 0)),
        ],
        out_shape=[
            jax.ShapeDtypeStruct((n, BIG_W), BF16),
            jax.ShapeDtypeStruct((n, SMALL_W), F32),
        ],
        scratch_shapes=[pltpu.VMEM((tm, d), BF16)],
        compiler_params=_cparams(("arbitrary", "arbitrary")),
        name="inproj",
    )(x2, shift, scale, w_big, ws_hi, ws_lo)


def _fcum_kernel(s_ref, bf_ref, o_ref, carry_ref):
    @pl.when(pl.program_id(1) == 0)
    def _():
        carry_ref[...] = jnp.zeros_like(carry_ref)

    lf = _log_sigmoid(s_ref[...] + bf_ref[...])
    rows = lax.broadcasted_iota(jnp.int32, lf.shape, 0)
    out = _cumsum_rows(lf, rows) + carry_ref[0:1, :]
    o_ref[...] = out
    carry_ref[0:1, :] = out[lf.shape[0] - 1:, :]


def _fcum_call(small, bf_row, bsz, seq):
    t = 512
    per = seq // t
    return pl.pallas_call(
        _fcum_kernel,
        grid=(bsz, per),
        in_specs=[
            pl.BlockSpec((t, SMALL_W), lambda b, i: (b * per + i, 0)),
            pl.BlockSpec((1, SMALL_W), lambda b, i: (0, 0)),
        ],
        out_specs=pl.BlockSpec((t, SMALL_W), lambda b, i: (b * per + i, 0)),
        out_shape=jax.ShapeDtypeStruct(small.shape, F32),
        scratch_shapes=[pltpu.VMEM((8, SMALL_W), F32)],
        compiler_params=_cparams(("arbitrary", "arbitrary")),
        name="forget_cumsum",
    )(small, bf_row)


def _split3_bf16(v):
    hi = v.astype(BF16)
    r = v - hi.astype(F32)
    mid = r.astype(BF16)
    lo = (r - mid.astype(F32)).astype(BF16)
    return hi, mid, lo


def _attn_kernel(q_ref, k_ref, v_ref, f_ref, o_ref, kx_ref, vx_ref, *, tq, tk):
    hp = pl.program_id(1)
    qi = pl.program_id(2)
    dh = ATT_HEAD_DIM
    seq = k_ref.shape[0]

    def head_lanes(rows, hh):
        lane = lax.broadcasted_iota(jnp.int32, (rows, LANES), 1)
        in_head = (lane >= hh * dh) & (lane < (hh + 1) * dh)
        return lane - (1 - hh) * dh, in_head

    def f_column(f_tile, hh):
        lane = lax.broadcasted_iota(jnp.int32, f_tile.shape, 1)
        return jnp.sum(jnp.where(lane == F_COL0 + 2 * hp + hh, f_tile, 0.0), axis=-1, keepdims=True)

    def f_terms(f_tile, hh):
        return [t.astype(F32) for t in _split3_bf16(f_column(f_tile, hh))]

    @pl.when(qi == 0)
    def _():
        f_all = f_ref[...]
        for hh in range(2):
            ext, in_head = head_lanes(seq, hh)
            hi, mid, lo = f_terms(f_all, hh)
            k_sp = jnp.where((ext >= 0) & (ext < 3), 1.0,
                             jnp.where(ext == 3, -hi, jnp.where(ext == 4, -mid,
                                                                jnp.where(ext == 5, -lo, 0.0))))
            kx_ref[hh] = jnp.where(in_head, k_ref[...], k_sp.astype(BF16))
            vx_ref[hh] = jnp.where(in_head, v_ref[...], jnp.where(ext == 0, 1.0, 0.0).astype(BF16))

    f_q = f_ref[pl.ds(pl.multiple_of(qi * tq, tq), tq), :]
    r_iota = lax.broadcasted_iota(jnp.int32, (tk, tk), 0)
    c_iota = lax.broadcasted_iota(jnp.int32, (tk, tk), 1)
    sub = tq // tk
    nt = (((1,), (1,)), ((), ()))
    qx = []
    for hh in range(2):
        ext, in_head = head_lanes(tq, hh)
        hi, mid, lo = f_terms(f_q, hh)
        q_sp = jnp.where(ext == 0, hi, jnp.where(ext == 1, mid, jnp.where(ext == 2, lo,
                         jnp.where((ext >= 3) & (ext < 6), 1.0, 0.0))))
        qx.append(jnp.where(in_head, q_ref[...] * 0.125, q_sp.astype(BF16)))

    def block(hh, j, state, diag):
        off = pl.multiple_of(j * tk, tk)
        first = 0 if diag is None else diag
        s = lax.dot_general(qx[hh][first * tk:], kx_ref[hh, pl.ds(off, tk), :], nt,
                            preferred_element_type=F32)
        new_m, probs = [], []
        for r in range(first, sub):
            s_r = s[(r - first) * tk:(r - first + 1) * tk]
            if diag is not None and r == diag:
                s_r = jnp.where(r_iota >= c_iota, s_r, -jnp.inf)
            s_max = jnp.max(s_r, axis=-1, keepdims=True)
            m_r = s_max if state[r] is None else jnp.maximum(state[r][0], s_max)
            new_m.append(m_r)
            probs.append(jnp.exp((s_r - m_r).astype(BF16)))
        pv = _dot(probs[0] if len(probs) == 1 else jnp.concatenate(probs, axis=0),
                  vx_ref[hh, pl.ds(off, tk), :])
        out = list(state)
        for r in range(first, sub):
            pv_r = pv[(r - first) * tk:(r - first + 1) * tk]
            m_r = new_m[r - first]
            out[r] = (m_r, pv_r if state[r] is None
                      else jnp.exp(state[r][0] - m_r) * state[r][1] + pv_r)
        return out

    st0 = st1 = [None] * sub
    for dg in range(sub):
        st0 = block(0, qi * sub + dg, st0, dg)
        st1 = block(1, qi * sub + dg, st1, dg)

    def body(j, carry):
        st0, st1 = carry
        return block(0, j, st0, None), block(1, j, st1, None)

    st0, st1 = lax.fori_loop(0, qi * sub, body, (st0, st1))
    a0 = jnp.concatenate([acc for _, acc in st0], axis=0) if sub > 1 else st0[0][1]
    a1 = jnp.concatenate([acc for _, acc in st1], axis=0) if sub > 1 else st1[0][1]
    out = None
    for hh, acc in ((0, a0), (1, a1)):
        ext, in_head = head_lanes(tq, hh)
        denom = jnp.sum(jnp.where(ext == 0, acc, 0.0), axis=-1, keepdims=True)
        o = acc / denom
        out = o if out is None else jnp.where(in_head, o, out)
    o_ref[...] = out.astype(BF16)


def _attn_call(big3, fcol):
    bsz, seq, _ = big3.shape
    tq = min(2048, seq)
    tk = min(512, tq)
    nq = seq // tq
    pairs = ATT_HEADS // 2
    kb, vb = COL_K // LANES, COL_V // LANES
    return pl.pallas_call(
        functools.partial(_attn_kernel, tq=tq, tk=tk),
        grid=(bsz, pairs, nq),
        in_specs=[
            pl.BlockSpec((None, tq, LANES), lambda b, h, i: (b, i, h)),
            pl.BlockSpec((None, seq, LANES), lambda b, h, i: (b, 0, kb + h)),
            pl.BlockSpec((None, seq, LANES), lambda b, h, i: (b, 0, vb + h)),
            pl.BlockSpec((seq, SMALL_W), lambda b, h, i: (b, 0)),
        ],
        out_specs=pl.BlockSpec((None, tq, LANES), lambda b, h, i: (b, i, h)),
        out_shape=jax.ShapeDtypeStruct((bsz, seq, ATT_WIDTH), BF16),
        scratch_shapes=[pltpu.VMEM((2, seq, LANES), BF16), pltpu.VMEM((2, seq, LANES), BF16)],
        compiler_params=_cparams(("arbitrary", "arbitrary", "arbitrary")),
        name="fox_attention",
    )(big3, big3, big3, fcol)


def _ssd_kernel(z_ref, xbc_ref, dt_ref, cw_ref, cb_ref, dtb_ref, alog_ref, dsk_ref, nw_ref,
                y_ref, ext_ref, state_ref):
    L = SSD_CHUNK
    P = SSM_HEAD_DIM
    GW = SSM_GROUP_WIDTH
    NS = SSM_STATE
    halo = 8

    @pl.when(pl.program_id(1) == 0)
    def _():
        ext_ref[0:halo, :] = jnp.zeros((halo, SSM_CONV_DIM), F32)
        state_ref[...] = jnp.zeros_like(state_ref)

    ext_ref[halo:halo + L, :] = xbc_ref[...].astype(F32)
    ext = ext_ref[...]
    conv = cb_ref[...] + cw_ref[SSM_CONV - 1:SSM_CONV, :] * ext[halo:, :]
    for lag in range(1, SSM_CONV):
        j = SSM_CONV - 1 - lag
        conv = conv + cw_ref[j:j + 1, :] * pltpu.roll(ext, lag, 0)[halo:, :]
    ext_ref[0:halo, :] = ext[L:L + halo, :]
    xc = _silu(conv)

    rows = lax.broadcasted_iota(jnp.int32, (L, LANES), 0)
    lane = lax.broadcasted_iota(jnp.int32, (L, LANES), 1)
    lo_half = lane < P
    dt = _softplus(dt_ref[...] + dtb_ref[...])
    a_cs = _cumsum_rows(dt * (-jnp.exp(alog_ref[...])), rows)
    a_t = jnp.concatenate([a_cs, a_cs], axis=0).T
    causal = rows >= jnp.where(lo_half, lane, lane - P)
    blk = (lax.broadcasted_iota(jnp.int32, (2 * L, LANES), 0) < L) == \
          (lax.broadcasted_iota(jnp.int32, (2 * L, LANES), 1) < P)

    def pair_cols(mat, h0):
        r = mat.shape[0]
        c0 = jnp.broadcast_to(mat[:, h0:h0 + 1], (r, LANES))
        c1 = jnp.broadcast_to(mat[:, h0 + 1:h0 + 2], (r, LANES))
        return jnp.where(lo_half[:r], c0, c1)

    def pair_rows(mat_t, h0):
        return jnp.where(lo_half[:1], mat_t[h0:h0 + 1, :], mat_t[h0 + 1:h0 + 2, :])

    for g in range(SSM_GROUPS):
        bm = xc[:, SSM_INNER + g * NS:SSM_INNER + (g + 1) * NS]
        cm = xc[:, SSM_INNER + SSM_GROUPS * NS + g * NS:SSM_INNER + SSM_GROUPS * NS + (g + 1) * NS]
        cm_b = cm.astype(BF16)
        bm_t2 = jnp.concatenate([bm, bm], axis=0).T.astype(BF16)
        cb2 = _dot(cm_b, bm_t2)
        st_prev = state_ref[g]
        y_off = _dot(cm_b, st_prev.astype(BF16))
        y_parts, xw_parts, cd_parts = [], [], []
        for pr in range(GW // LANES):
            h0 = g * (GW // P) + 2 * pr
            c0 = g * GW + pr * LANES
            xs_p = xc[:, c0:c0 + LANES]
            a_col = pair_cols(a_cs, h0)
            a_end = a_col[L - 1:L, :]
            seg = a_col - pair_rows(a_t, h0)
            decay = jnp.where(causal, jnp.exp(jnp.where(causal, seg, 0.0)), 0.0)
            xdt = xs_p * pair_cols(dt, h0)
            xdt2 = jnp.concatenate([xdt, xdt], axis=0)
            xdt_bd = jnp.where(blk, xdt2, 0.0).astype(BF16)
            y_d = _dot((cb2 * decay).astype(BF16), xdt_bd)
            y_o = y_off[:, pr * LANES:(pr + 1) * LANES] * jnp.exp(a_col)
            y_parts.append(y_d + y_o + dsk_ref[:, c0:c0 + LANES] * xs_p)
            xw_parts.append(xdt * jnp.exp(a_end - a_col))
            cd_parts.append(jnp.exp(a_end))
        xw = jnp.concatenate(xw_parts, axis=-1).astype(BF16)
        state_ref[g] = st_prev * jnp.concatenate(cd_parts, axis=-1) + _dot(bm_t2[:, :L], xw)
        y = jnp.concatenate(y_parts, axis=-1) * _silu(z_ref[:, g * GW:(g + 1) * GW].astype(F32))
        y = y * lax.rsqrt(jnp.mean(y * y, axis=-1, keepdims=True) + RMS_EPS)
        y_ref[:, g * GW:(g + 1) * GW] = (y * nw_ref[:, g * GW:(g + 1) * GW]).astype(BF16)


def _ssd_call(big, small, conv_w, conv_b, dtb_row, alog_row, dsk_row, nw_row, bsz, seq):
    n = big.shape[0]
    L = SSD_CHUNK
    nc = seq // L
    const = lambda b, c: (0, 0)
    return pl.pallas_call(
        _ssd_kernel,
        grid=(bsz, nc),
        in_specs=[
            pl.BlockSpec((L, SSM_INNER), lambda b, c: (b * nc + c, COL_Z // SSM_INNER)),
            pl.BlockSpec((L, SSM_CONV_DIM), lambda b, c: (b * nc + c, COL_XBC // SSM_CONV_DIM)),
            pl.BlockSpec((L, SMALL_W), lambda b, c: (b * nc + c, 0)),
            pl.BlockSpec((SSM_CONV, SSM_CONV_DIM), const),
            pl.BlockSpec((1, SSM_CONV_DIM), const),
            pl.BlockSpec((1, SMALL_W), const),
            pl.BlockSpec((1, SMALL_W), const),
            pl.BlockSpec((1, SSM_INNER), const),
            pl.BlockSpec((1, SSM_INNER), const),
        ],
        out_specs=pl.BlockSpec((L, SSM_INNER), lambda b, c: (b * nc + c, 0)),
        out_shape=jax.ShapeDtypeStruct((n, SSM_INNER), BF16),
        scratch_shapes=[
            pltpu.VMEM((8 + L, SSM_CONV_DIM), F32),
            pltpu.VMEM((SSM_GROUPS, SSM_STATE, SSM_GROUP_WIDTH), F32),
        ],
        compiler_params=_cparams(("arbitrary", "arbitrary")),
        name="ssd_scan",
    )(big, big, small, conv_w, conv_b, dtb_row, alog_row, dsk_row, nw_row)


def _mixout_kernel(x_ref, ya_ref, ys_ref, u_ref, uh_ref, gate_ref, gm_ref, shf_ref, scf_ref,
                   wao_ref, wso_ref, wp_ref, ps_ref, wout_ref, lng_ref, lnb_ref, wrh_ref, wrl_ref,
                   x1_ref, h2_ref, lt_ref, ext_ref, *, tm, tiles_per_seq):
    i = pl.program_id(0)
    halo = 16
    first = (i % tiles_per_seq) == 0
    ext_ref[0:halo, :] = jnp.where(first, 0.0, uh_ref[...].astype(F32))
    u = u_ref[...].astype(F32)
    ext_ref[halo:halo + tm, :] = u
    pos = ((i % tiles_per_seq) * tm + 1 + lax.broadcasted_iota(jnp.int32, (tm, 1), 0)).astype(F32)

    y_att = _dot(ya_ref[...], wao_ref[...])
    y_ssm = _dot(ys_ref[...], wso_ref[...])
    pool_parts = []
    for g, w in enumerate(POOL_WINDOWS):
        c0 = g * POOL_GROUP_DIM
        s = ext_ref[:, c0:c0 + POOL_GROUP_DIM]
        k = 1
        while k < w:
            s = s + pltpu.roll(s, k, 0)
            k *= 2
        pooled = s[halo:, :] / jnp.minimum(pos, float(w)) - u[:, c0:c0 + POOL_GROUP_DIM]
        pool_parts.append(_dot(pooled.astype(BF16), wp_ref[g]))
    y_pool = jnp.concatenate(pool_parts, axis=-1) * ps_ref[...]

    g_att = _sigmoid(gate_ref[:, 0:D_MODEL].astype(F32))
    g_pool = _sigmoid(gate_ref[:, D_MODEL:2 * D_MODEL].astype(F32))
    g_ssm = _sigmoid(gate_ref[:, 2 * D_MODEL:3 * D_MODEL].astype(F32))
    merged = g_att * y_att + g_pool * y_pool + g_ssm * y_ssm
    mix = _dot(merged.astype(BF16), wout_ref[...])
    x1 = _layer_norm(DEEPNORM_ALPHA * x_ref[...] + gm_ref[0] * mix) * lng_ref[...] + lnb_ref[...]
    x1_ref[...] = x1
    h2 = _layer_norm(x1) * (1.0 + scf_ref[0]) + shf_ref[0]
    _store_token_major(h2_ref, h2)
    h_hi, h_lo = _split_bf16(h2)
    nt = (((1,), (1,)), ((), ()))
    lt_ref[...] = (lax.dot_general(wrh_ref[...], h_hi, nt, preferred_element_type=F32)
                   + (lax.dot_general(wrh_ref[...], h_lo, nt, preferred_element_type=F32)
                      + lax.dot_general(wrl_ref[...], h_hi, nt, preferred_element_type=F32)))


def _mixout_call(x2, y_att, y_ssm, big, gm, shf, scf, wao, wso, wp, ps_row, wout, lng, lnb,
                 wr_hi, wr_lo, seq):
    n, d = x2.shape
    tm = 512
    halo = 16
    tiles_per_seq = seq // tm
    const2 = lambda i: (0, 0)
    per_seq = lambda i: (i // tiles_per_seq, 0, 0)
    return pl.pallas_call(
        functools.partial(_mixout_kernel, tm=tm, tiles_per_seq=tiles_per_seq),
        grid=(n // tm,),
        in_specs=[
            pl.BlockSpec((tm, d), lambda i: (i, 0)),
            pl.BlockSpec((tm, ATT_WIDTH), lambda i: (i, 0)),
            pl.BlockSpec((tm, SSM_INNER), lambda i: (i, 0)),
            pl.BlockSpec((tm, POOL_WIDTH), lambda i: (i, COL_POOL // POOL_WIDTH)),
            pl.BlockSpec((halo, POOL_WIDTH),
                         lambda i: (jnp.maximum(i * (tm // halo) - 1, 0), COL_POOL // POOL_WIDTH)),
            pl.BlockSpec((tm, N_BRANCHES * d), lambda i: (i, COL_GATE // (N_BRANCHES * d))),
            pl.BlockSpec((1, 1, d), per_seq),
            pl.BlockSpec((1, 1, d), per_seq),
            pl.BlockSpec((1, 1, d), per_seq),
            pl.BlockSpec((ATT_WIDTH, d), const2),
            pl.BlockSpec((SSM_INNER, d), const2),
            pl.BlockSpec((POOL_GROUPS, POOL_GROUP_DIM, POOL_GROUP_DIM), lambda i: (0, 0, 0)),
            pl.BlockSpec((1, d), const2),
            pl.BlockSpec((d, d), const2),
            pl.BlockSpec((1, d), const2),
            pl.BlockSpec((1, d), const2),
            pl.BlockSpec((N_EXPERTS, d), const2),
            pl.BlockSpec((N_EXPERTS, d), const2),
        ],
        out_specs=[
            pl.BlockSpec((tm, d), lambda i: (i, 0)),
            pl.BlockSpec((tm * (d // LANES), LANES), lambda i: (i, 0)),
            pl.BlockSpec((N_EXPERTS, tm), lambda i: (0, i)),
        ],
        out_shape=[
            jax.ShapeDtypeStruct((n, d), F32),
            jax.ShapeDtypeStruct((n * (d // LANES), LANES), F32),
            jax.ShapeDtypeStruct((N_EXPERTS, n), F32),
        ],
        scratch_shapes=[pltpu.VMEM((halo + tm, POOL_WIDTH), F32)],
        compiler_params=_cparams(("arbitrary",)),
        name="mixer_out",
    )(x2, y_att, y_ssm, big, big, big, gm, shf, scf, wao, wso, wp, ps_row, wout, lng, lnb,
      wr_hi, wr_lo)


def _top2(vals):
    n = len(vals)
    v1 = vals[0]
    for v in vals[1:]:
        v1 = jnp.maximum(v1, v)
    i1 = jnp.full(v1.shape, n, jnp.int32)
    for j in reversed(range(n)):
        i1 = jnp.where(vals[j] == v1, j, i1)
    v2 = jnp.full(v1.shape, -jnp.inf, F32)
    for j in range(n):
        v2 = jnp.maximum(v2, jnp.where(i1 == j, -jnp.inf, vals[j]))
    i2 = jnp.full(v1.shape, n, jnp.int32)
    for j in reversed(range(n)):
        i2 = jnp.where((vals[j] == v2) & (i1 != j), j, i2)
    return v1, i1, v2, i2


def _router_kernel(lt_ref, br_ref, e_ref, w_ref, cnt_ref):
    lg = lt_ref[...]
    m = jnp.max(lg, axis=0, keepdims=True)
    ex = jnp.exp(lg - m)
    probs = ex / jnp.sum(ex, axis=0, keepdims=True)
    sel = probs + br_ref[...]
    tops = []
    for g in range(N_EXPERT_GROUPS):
        vals = [sel[g * EXPERTS_PER_GROUP + j:g * EXPERTS_PER_GROUP + j + 1, :]
                for j in range(EXPERTS_PER_GROUP)]
        tops.append(_top2(vals))
    best = tops[0][0] + tops[0][2]
    e1 = tops[0][1]
    e2 = tops[0][3]
    for g in range(1, N_EXPERT_GROUPS):
        score = tops[g][0] + tops[g][2]
        better = score > best
        best = jnp.where(better, score, best)
        e1 = jnp.where(better, tops[g][1] + g * EXPERTS_PER_GROUP, e1)
        e2 = jnp.where(better, tops[g][3] + g * EXPERTS_PER_GROUP, e2)
    p1 = jnp.zeros_like(best)
    p2 = jnp.zeros_like(best)
    for e in range(N_EXPERTS):
        p1 = jnp.where(e1 == e, probs[e:e + 1, :], p1)
        p2 = jnp.where(e2 == e, probs[e:e + 1, :], p2)
    tot = p1 + p2
    e_ref[...] = jnp.concatenate([e1, e2], axis=0)
    w_ref[...] = jnp.concatenate([p1 / tot, p2 / tot], axis=0)

    @pl.when(pl.program_id(0) == 0)
    def _():
        cnt_ref[...] = jnp.zeros_like(cnt_ref)

    e_rows = lax.broadcasted_iota(jnp.int32, lg.shape, 0)
    hits = ((e_rows == e1) | (e_rows == e2)).astype(jnp.int32)
    cnt_ref[...] += jnp.sum(hits, axis=1, keepdims=True)


def _router_call(logits_t, br_col):
    e, n = logits_t.shape
    t = min(2048, n)
    return pl.pallas_call(
        _router_kernel,
        grid=(n // t,),
        in_specs=[
            pl.BlockSpec((e, t), lambda i: (0, i)),
            pl.BlockSpec((e, 1), lambda i: (0, 0)),
        ],
        out_specs=[
            pl.BlockSpec((TOP_K, t), lambda i: (0, i)),
            pl.BlockSpec((TOP_K, t), lambda i: (0, i)),
            pl.BlockSpec((e, LANES), lambda i: (0, 0)),
        ],
        out_shape=[
            jax.ShapeDtypeStruct((TOP_K, n), jnp.int32),
            jax.ShapeDtypeStruct((TOP_K, n), F32),
            jax.ShapeDtypeStruct((e, LANES), jnp.int32),
        ],
        compiler_params=_cparams(("arbitrary",)),
        name="router_top2",
    )(logits_t, br_col)


def _dispatch_kernel(dest_ref, h_ref, zero_hbm, x_hbm, sem):
    del zero_hbm
    c = D_MODEL // LANES
    t = h_ref.shape[0] // c
    base = pl.program_id(0) * (t * TOP_K)
    for k in range(t):
        for j in range(TOP_K):
            row = pl.multiple_of(dest_ref[base + k * TOP_K + j], c)
            pltpu.make_async_copy(h_ref.at[pl.ds(k * c, c), :], x_hbm.at[pl.ds(row, c), :],
                                  sem).start(priority=j % DMA_PRIORITIES)
    for j in range(TOP_K):
        pltpu.make_async_copy(h_ref, x_hbm.at[pl.ds(0, t * c), :], sem).wait()


def _dispatch_call(dest_rows, h2_tiles, p):
    rows, _ = h2_tiles.shape
    c = D_MODEL // LANES
    t = min(256, rows // c)
    grid_spec = pltpu.PrefetchScalarGridSpec(
        num_scalar_prefetch=1,
        grid=(rows // (t * c),),
        in_specs=[pl.BlockSpec((t * c, LANES), lambda i, dest: (i, 0)),
                  pl.BlockSpec(memory_space=pl.ANY)],
        out_specs=pl.BlockSpec(memory_space=pl.ANY),
        scratch_shapes=[pltpu.SemaphoreType.DMA(())],
    )
    return pl.pallas_call(
        _dispatch_kernel,
        grid_spec=grid_spec,
        out_shape=jax.ShapeDtypeStruct((p * c, LANES), F32),
        input_output_aliases={2: 0},
        compiler_params=_cparams(("arbitrary",)),
        name="moe_dispatch",
    )(dest_rows, h2_tiles, jnp.zeros((p * c, LANES), F32))


def _expert_kernel(dst_ref, bexp_ref,
                   w1a_ref, w3a_ref, w2a_ref, swa_ref, xa, w1b_ref, w3b_ref, w2b_ref, swb_ref, xb,
                   y_hbm, ya, yb, w13a_bf, w2a_bf, w13b_bf, w2b_bf, ssem):
    i = pl.program_id(0)
    last = pl.num_programs(0) - 1
    c = w1a_ref.shape[1] // LANES

    def start_scatter(blk, ybuf, sem):
        base = (blk + 1) * MOE_BLOCK
        for k in range(MOE_BLOCK):
            row = pl.multiple_of(dst_ref[base + k], c)
            pltpu.make_async_copy(ybuf.at[pl.ds(k * c, c), :], y_hbm.at[pl.ds(row, c), :],
                                  sem).start(priority=k % DMA_PRIORITIES)

    def wait_scatter(ybuf, sem):
        pltpu.make_async_copy(ybuf, y_hbm.at[pl.ds(0, MOE_BLOCK * c), :], sem).wait()

    def ffn(xbuf, w_bf, sw_ref, ybuf):
        w13_bf, w2_bf = w_bf
        xv = _load_token_major(xbuf, MOE_BLOCK, c).astype(BF16)
        hid = _silu(_dot(xv, w13_bf[0])) * _dot(xv, w13_bf[1])
        _store_token_major(ybuf, _dot(hid.astype(BF16), w2_bf[...]) * sw_ref[...])

    def refresh_weights(parity, w1_ref, w3_ref, w2_ref, w_bf):
        w13_bf, w2_bf = w_bf
        blk = 2 * i + parity
        changed = (i == 0) | (bexp_ref[blk] != bexp_ref[jnp.maximum(blk - 2, 0)])

        @pl.when(changed)
        def _():
            w13_bf[0] = w1_ref[0].astype(BF16)
            w13_bf[1] = w3_ref[0].astype(BF16)
            w2_bf[...] = w2_ref[0].astype(BF16)

    @pl.when(i == 0)
    def _():
        yb[...] = jnp.zeros_like(yb)

    @pl.when(i > 0)
    def _():
        wait_scatter(ya, ssem.at[0])

    refresh_weights(0, w1a_ref, w3a_ref, w2a_ref, (w13a_bf, w2a_bf))
    refresh_weights(1, w1b_ref, w3b_ref, w2b_ref, (w13b_bf, w2b_bf))

    start_scatter(2 * i - 1, yb, ssem.at[1])
    ffn(xa, (w13a_bf, w2a_bf), swa_ref, ya)

    wait_scatter(yb, ssem.at[1])
    start_scatter(2 * i, ya, ssem.at[0])
    ffn(xb, (w13b_bf, w2b_bf), swb_ref, yb)

    @pl.when(i == last)
    def _():
        start_scatter(2 * i + 1, yb, ssem.at[1])
        wait_scatter(ya, ssem.at[0])
        wait_scatter(yb, ssem.at[1])


def _expert_call(dst_ext, blk_expert, x_sorted, w1, w3, w2, layer, slot_w):
    d = w1.shape[2]
    c = d // LANES
    p = slot_w.shape[0]
    nblk = p // MOE_BLOCK
    assert nblk % 2 == 0 and dst_ext.shape[0] == p + MOE_BLOCK

    def weight_spec(shape, parity):
        return pl.BlockSpec(shape, lambda i, dst, be: (layer, be[2 * i + parity], 0, 0))

    def block_specs(parity):
        return [weight_spec((None, 1, d, EXPERT_DFF), parity), weight_spec((None, 1, d, EXPERT_DFF), parity),
                weight_spec((None, 1, EXPERT_DFF, d), parity),
                pl.BlockSpec((MOE_BLOCK, 1), lambda i, dst, be: (2 * i + parity, 0)),
                pl.BlockSpec((MOE_BLOCK * c, LANES), lambda i, dst, be: (2 * i + parity, 0))]

    grid_spec = pltpu.PrefetchScalarGridSpec(
        num_scalar_prefetch=2,
        grid=(nblk // 2,),
        in_specs=block_specs(0) + block_specs(1),
        out_specs=pl.BlockSpec(memory_space=pl.ANY),
        scratch_shapes=[pltpu.VMEM((MOE_BLOCK * c, LANES), F32)] * 2
        + [pltpu.VMEM((2, d, EXPERT_DFF), BF16), pltpu.VMEM((EXPERT_DFF, d), BF16)] * 2
        + [pltpu.SemaphoreType.DMA((2,))],
    )
    return pl.pallas_call(
        _expert_kernel,
        grid_spec=grid_spec,
        out_shape=jax.ShapeDtypeStruct(((p + MOE_BLOCK) * c, LANES), F32),
        compiler_params=_cparams(("arbitrary",)),
        name="moe_experts",
    )(dst_ext, blk_expert, w1, w3, w2, slot_w, x_sorted, w1, w3, w2, slot_w, x_sorted)


def _combine_kernel(x_ref, y_ref, gf_ref, lng_ref, lnb_ref, o_ref):
    tm, d = x_ref.shape
    c = d // LANES
    ffn = _load_token_major(y_ref, tm, c, 0, TOP_K) + _load_token_major(y_ref, tm, c, 1, TOP_K)
    o_ref[...] = _layer_norm(DEEPNORM_ALPHA * x_ref[...] + gf_ref[0] * ffn) * lng_ref[...] + lnb_ref[...]


def _combine_call(x1, y_pairs, gf, lng, lnb, seq):
    n, d = x1.shape
    tm = 512
    tiles_per_seq = seq // tm
    return pl.pallas_call(
        _combine_kernel,
        grid=(n // tm,),
        in_specs=[
            pl.BlockSpec((tm, d), lambda i: (i, 0)),
            pl.BlockSpec((tm * TOP_K * (d // LANES), LANES), lambda i: (i, 0)),
            pl.BlockSpec((1, 1, d), lambda i: (i // tiles_per_seq, 0, 0)),
            pl.BlockSpec((1, d), lambda i: (0, 0)),
            pl.BlockSpec((1, d), lambda i: (0, 0)),
        ],
        out_specs=pl.BlockSpec((tm, d), lambda i: (i, 0)),
        out_shape=jax.ShapeDtypeStruct((n, d), F32),
        compiler_params=_cparams(("arbitrary",)),
        name="moe_combine",
    )(x1, y_pairs, gf, lng, lnb)


def _slot_tables(e_idx, gate_w, counts, n, tile_rows):
    nk = n * TOP_K
    e_flat = e_idx.reshape(nk)
    w_flat = gate_w.reshape(nk)
    order = jnp.argsort(e_flat).astype(jnp.int32)
    starts = jnp.cumsum(counts) - counts
    padded = ((counts + MOE_BLOCK - 1) // MOE_BLOCK) * MOE_BLOCK
    pends = jnp.cumsum(padded)
    pstarts = pends - padded
    p = nk + N_EXPERTS * MOE_BLOCK
    nblk = p // MOE_BLOCK
    blk_start = jnp.arange(nblk, dtype=jnp.int32) * MOE_BLOCK
    blk_expert = jnp.minimum(jnp.sum(pends[None, :] <= blk_start[:, None], axis=1),
                             N_EXPERTS - 1).astype(jnp.int32)
    per_slot = lambda tbl: jnp.repeat(tbl[blk_expert], MOE_BLOCK)
    rank = jnp.arange(p, dtype=jnp.int32) - per_slot(pstarts)
    is_pad = rank >= per_slot(counts)
    slot_flat = order[jnp.minimum(per_slot(starts) + rank, nk - 1)]
    slot_w = jnp.where(is_pad, 0.0, w_flat[slot_flat])
    pads_before = per_slot(pstarts - starts) + rank - per_slot(counts)
    slot_dst = jnp.where(is_pad, nk + pads_before, slot_flat)
    dst_ext = jnp.concatenate([p + jnp.arange(MOE_BLOCK, dtype=jnp.int32), slot_dst])
    pos = jnp.argsort(order).astype(jnp.int32)
    dest = pos + (pstarts - starts)[e_flat]
    return dest * tile_rows, dst_ext * tile_rows, blk_expert, slot_w.reshape(p, 1)


def kernel(x, c, w_in, b_forget, w_attn_o, w_pool, pool_scale, conv_w, conv_b, dt_bias, a_log,
           d_skip, ssm_norm_w, w_ssm_o, w_out, w_ada, b_ada, ln_mix_g, ln_mix_b, ln_ffn_g,
           ln_ffn_b, w_router, b_router, w_exp_gate, w_exp_up, w_exp_down):
    bsz, seq, d = x.shape
    n = bsz * seq
    mod = _ada_call(c, w_ada, b_ada)
    wr_hi, wr_lo = _split_bf16(w_router.T)
    br_col = b_router.reshape(N_EXPERTS, 1)
    x2 = x.reshape(n, d)
    w_big = _w_in_call(w_in)
    o_f = 3 * ATT_WIDTH
    o_dt = o_f + ATT_HEADS + POOL_WIDTH + SSM_INNER + SSM_CONV_DIM
    w_small = jnp.concatenate(
        [w_in[:, :, o_dt:o_dt + SSM_HEADS], w_in[:, :, o_f:o_f + ATT_HEADS],
         jnp.zeros((DEPTH, d, SMALL_W - SSM_HEADS - ATT_HEADS), F32)], axis=2)
    ws_hi, ws_lo = _split_bf16(w_small)
    for l in range(DEPTH):
        sh_m, sc_m, g_m, sh_f, sc_f, g_f = (mod[l, :, i * d:(i + 1) * d].reshape(bsz, 1, d)
                                            for i in range(6))
        big, small = _inproj_call(x2, sh_m, sc_m, w_big, ws_hi, ws_lo, l, seq)

        bf_row = jnp.zeros((1, SMALL_W), F32).at[0, F_COL0:F_COL0 + ATT_HEADS].set(b_forget[l])
        fcol = _fcum_call(small, bf_row, bsz, seq)
        y_att = _attn_call(big.reshape(bsz, seq, BIG_W), fcol).reshape(n, ATT_WIDTH)

        pad_heads = jnp.zeros((SMALL_W - SSM_HEADS,), F32)
        dtb_row = jnp.concatenate([dt_bias[l], pad_heads]).reshape(1, SMALL_W)
        alog_row = jnp.concatenate([a_log[l], pad_heads]).reshape(1, SMALL_W)
        dsk_row = jnp.repeat(d_skip[l], SSM_HEAD_DIM).reshape(1, SSM_INNER)
        y_ssm = _ssd_call(big, small, conv_w[l], conv_b[l].reshape(1, SSM_CONV_DIM), dtb_row,
                          alog_row, dsk_row, ssm_norm_w[l].reshape(1, SSM_INNER), bsz, seq)

        x1, h2, logits_t = _mixout_call(
            x2, y_att, y_ssm, big, g_m, sh_f, sc_f, w_attn_o[l].astype(BF16),
            w_ssm_o[l].astype(BF16), w_pool[l].astype(BF16), pool_scale[l].reshape(1, d),
            w_out[l].astype(BF16), ln_mix_g[l].reshape(1, d), ln_mix_b[l].reshape(1, d),
            wr_hi, wr_lo, seq)

        e_idx_t, gate_t, counts = _router_call(logits_t, br_col)
        dest, dst_ext, blk_expert, slot_w = _slot_tables(e_idx_t.T, gate_t.T, counts[:, 0], n,
                                                         d // LANES)
        x_sorted = _dispatch_call(dest, h2, slot_w.shape[0])
        y_pairs = _expert_call(dst_ext, blk_expert, x_sorted,
                               w_exp_gate, w_exp_up, w_exp_down, l, slot_w)
        x2 = _combine_call(x1, y_pairs, g_f, ln_ffn_g[l].reshape(1, d), ln_ffn_b[l].reshape(1, d), seq)
    return x2.reshape(bsz, seq, d)
```

```python
import functools

import jax
import jax.numpy as jnp
from jax import lax
from jax.experimental import pallas as pl
from jax.experimental.pallas import tpu as pltpu

F32 = jnp.float32
BF16 = jnp.bfloat16

D_MODEL = 1024
DEPTH = 2
ATT_HEADS = 16
ATT_HEAD_DIM = 64
ATT_WIDTH = ATT_HEADS * ATT_HEAD_DIM
POOL_WINDOWS = (2, 4, 8, 16)
POOL_GROUPS = len(POOL_WINDOWS)
POOL_WIDTH = D_MODEL
POOL_GROUP_DIM = POOL_WIDTH // POOL_GROUPS
SSM_INNER = 2 * D_MODEL
SSM_HEAD_DIM = 64
SSM_HEADS = SSM_INNER // SSM_HEAD_DIM
SSM_GROUPS = 4
SSM_STATE = 128
SSM_CONV = 4
SSM_CONV_DIM = SSM_INNER + 2 * SSM_GROUPS * SSM_STATE
SSM_GROUP_WIDTH = SSM_INNER // SSM_GROUPS
N_BRANCHES = 3
N_EXPERTS = 16
N_EXPERT_GROUPS = 4
EXPERTS_PER_GROUP = N_EXPERTS // N_EXPERT_GROUPS
TOP_K = 2
EXPERT_DFF = 512
MOE_BLOCK = 256
DEEPNORM_ALPHA = (2 * DEPTH) ** 0.25
LN_EPS = 1e-5
RMS_EPS = 1e-6

LANES = 128
SSD_CHUNK = 64
SSD_CHUNKS_PER_STEP = 2
DMA_PRIORITIES = 2
SMALL_W = LANES
F_COL0 = SSM_HEADS
COL_Q, COL_K, COL_V = 0, ATT_WIDTH, 2 * ATT_WIDTH
COL_POOL = 3 * ATT_WIDTH
COL_Z = COL_POOL + POOL_WIDTH
COL_XBC = COL_Z + SSM_INNER
COL_GATE = COL_XBC + SSM_CONV_DIM
BIG_W = COL_GATE + N_BRANCHES * D_MODEL
VMEM_LIMIT = 48 * 1024 * 1024


def _cparams(sem):
    return pltpu.CompilerParams(dimension_semantics=sem, vmem_limit_bytes=VMEM_LIMIT)


def _sigmoid(v):
    return 0.5 + 0.5 * jnp.tanh(0.5 * v)


def _silu(v):
    return v * _sigmoid(v)


def _softplus(v):
    return jnp.maximum(v, 0.0) + jnp.log1p(jnp.exp(-jnp.abs(v)))


def _log_sigmoid(v):
    return jnp.minimum(v, 0.0) - jnp.log1p(jnp.exp(-jnp.abs(v)))


def _layer_norm(v):
    mu = jnp.mean(v, axis=-1, keepdims=True)
    vc = v - mu
    return vc * lax.rsqrt(jnp.mean(vc * vc, axis=-1, keepdims=True) + LN_EPS)


def _split_bf16(v):
    hi = v.astype(BF16)
    lo = (v - hi.astype(F32)).astype(BF16)
    return hi, lo


def _dot(a, b):
    return jnp.dot(a, b, preferred_element_type=F32)


def _dot3(a_hi, a_lo, b_hi, b_lo):
    return _dot(a_hi, b_hi) + (_dot(a_lo, b_hi) + _dot(a_hi, b_lo))


def _store_token_major(ref, v, group=0, groups=1):
    t, width = v.shape
    c = width // LANES
    for s in range(c):
        ref[pl.ds(group * c + s, t, stride=groups * c), :] = v[:, s * LANES:(s + 1) * LANES]


def _load_token_major(ref, t, c, group=0, groups=1):
    return jnp.concatenate(
        [ref[pl.ds(group * c + s, t, stride=groups * c), :] for s in range(c)], axis=-1)


def _cumsum_rows(v, rows):
    t = v.shape[0]
    k = 1
    while k < t:
        v = v + jnp.where(rows >= k, pltpu.roll(v, k, 0), 0.0)
        k *= 2
    return v


def _ada_kernel(c_ref, w_ref, b_ref, o_ref):
    cond = _silu(c_ref[...])
    c_hi, c_lo = _split_bf16(cond)
    w_hi, w_lo = _split_bf16(w_ref[0])
    o_ref[0] = _dot3(c_hi, c_lo, w_hi, w_lo) + b_ref[0]


def _ada_call(c, w_ada, b_ada):
    depth, d, width = w_ada.shape
    bsz = c.shape[0]
    rows = 8
    c_pad = jnp.zeros((rows, d), F32).at[:bsz].set(c)
    tn = 1024
    out = pl.pallas_call(
        _ada_kernel,
        grid=(depth, width // tn),
        in_specs=[
            pl.BlockSpec((rows, d), lambda l, j: (0, 0)),
            pl.BlockSpec((1, d, tn), lambda l, j: (l, 0, j)),
            pl.BlockSpec((1, 1, tn), lambda l, j: (l, 0, j)),
        ],
        out_specs=pl.BlockSpec((1, rows, tn), lambda l, j: (l, 0, j)),
        out_shape=jax.ShapeDtypeStruct((depth, rows, width), F32),
        compiler_params=_cparams(("arbitrary", "arbitrary")),
        name="ada_mod",
    )(c_pad, w_ada, b_ada.reshape(depth, 1, width))
    return out[:, :bsz]


W_IN_TN = 1024
W_IN_SKIPS = ((3 * ATT_WIDTH // W_IN_TN, ATT_HEADS),
              (COL_GATE // W_IN_TN, ATT_HEADS + SSM_HEADS))


def _w_in_kernel(a_ref, b_ref, o_ref):
    jb = pl.program_id(1)
    (j1, s1), (j2, s2) = W_IN_SKIPS

    def emit(shift):
        if shift == 0:
            o_ref[...] = a_ref[...].astype(BF16)
        else:
            both = jnp.concatenate([a_ref[...], b_ref[...]], axis=1)
            o_ref[...] = both[:, shift:shift + W_IN_TN].astype(BF16)

    pl.when(jb < j1)(lambda: emit(0))
    pl.when((jb >= j1) & (jb < j2))(lambda: emit(s1))
    pl.when(jb >= j2)(lambda: emit(s2))


def _w_in_call(w_in):
    depth, d, _ = w_in.shape
    per = W_IN_TN // LANES
    return pl.pallas_call(
        _w_in_kernel,
        grid=(depth, BIG_W // W_IN_TN),
        in_specs=[
            pl.BlockSpec((None, d, W_IN_TN), lambda l, j: (l, 0, j)),
            pl.BlockSpec((None, d, LANES), lambda l, j: (l, 0, (j + 1) * per)),
        ],
        out_specs=pl.BlockSpec((None, d, W_IN_TN), lambda l, j: (l, 0, j)),
        out_shape=jax.ShapeDtypeStruct((depth, d, BIG_W), BF16),
        compiler_params=_cparams(("arbitrary", "arbitrary")),
        name="w_in_relayout",
    )(w_in, w_in)


def _inproj_kernel(x_ref, sh_ref, sc_ref, w_ref, wsh_ref, wsl_ref, big_ref, small_ref, h_ref):
    @pl.when(pl.program_id(1) == 0)
    def _():
        h = _layer_norm(x_ref[...]) * (1.0 + sc_ref[0]) + sh_ref[0]
        h_hi, h_lo = _split_bf16(h)
        h_ref[...] = h_hi
        small_ref[...] = _dot3(h_hi, h_lo, wsh_ref[...], wsl_ref[...])

    big_ref[...] = _dot(h_ref[...], w_ref[...]).astype(BF16)


def _inproj_call(x2, shift, scale, w_big, ws_hi, ws_lo, layer, seq):
    n, d = x2.shape
    tm, tn = min(1024, seq), 1024
    tiles_per_seq = seq // tm
    return pl.pallas_call(
        _inproj_kernel,
        grid=(n // tm, BIG_W // tn),
        in_specs=[
            pl.BlockSpec((tm, d), lambda i, j: (i, 0)),
            pl.BlockSpec((1, 1, d), lambda i, j: (i // tiles_per_seq, 0, 0)),
            pl.BlockSpec((1, 1, d), lambda i, j: (i // tiles_per_seq, 0, 0)),
            pl.BlockSpec((None, d, tn), lambda i, j: (layer, 0, j)),
            pl.BlockSpec((None, d, SMALL_W), lambda i, j: (layer, 0, 0)),
            pl.BlockSpec((None, d, SMALL_W), lambda i, j: (layer, 0, 0)),
        ],
        out_specs=[
            pl.BlockSpec((tm, tn), lambda i, j: (i, j)),
            pl.BlockSpec((tm, SMALL_W), lambda i, j: (i, 0)),
        ],
        out_shape=[
            jax.ShapeDtypeStruct((n, BIG_W), BF16),
            jax.ShapeDtypeStruct((n, SMALL_W), F32),
        ],
        scratch_shapes=[pltpu.VMEM((tm, d), BF16)],
        compiler_params=_cparams(("arbitrary", "arbitrary")),
        name="inproj",
    )(x2, shift, scale, w_big, ws_hi, ws_lo)


def _fcum_kernel(s_ref, bf_ref, o_ref, carry_ref):
    @pl.when(pl.program_id(1) == 0)
    def _():
        carry_ref[...] = jnp.zeros_like(carry_ref)

    lf = _log_sigmoid(s_ref[...] + bf_ref[...])
    rows = lax.broadcasted_iota(jnp.int32, lf.shape, 0)
    out = _cumsum_rows(lf, rows) + carry_ref[0:1, :]
    o_ref[...] = out
    carry_ref[0:1, :] = out[lf.shape[0] - 1:, :]


def _fcum_call(small, bf_row, bsz, seq):
    t = 512
    per = seq // t
    return pl.pallas_call(
        _fcum_kernel,
        grid=(bsz, per),
        in_specs=[
            pl.BlockSpec((t, SMALL_W), lambda b, i: (b * per + i, 0)),
            pl.BlockSpec((1, SMALL_W), lambda b, i: (0, 0)),
        ],
        out_specs=pl.BlockSpec((t, SMALL_W), lambda b, i: (b * per + i, 0)),
        out_shape=jax.ShapeDtypeStruct(small.shape, F32),
        scratch_shapes=[pltpu.VMEM((8, SMALL_W), F32)],
        compiler_params=_cparams(("arbitrary", "arbitrary")),
        name="forget_cumsum",
    )(small, bf_row)


def _split3_bf16(v):
    hi = v.astype(BF16)
    r = v - hi.astype(F32)
    mid = r.astype(BF16)
    lo = (r - mid.astype(F32)).astype(BF16)
    return hi, mid, lo


def _attn_kernel(q_ref, k_ref, v_ref, f_ref, o_ref, kx_ref, vx_ref, *, tq, tk):
    hp = pl.program_id(1)
    qi = pl.program_id(2)
    dh = ATT_HEAD_DIM
    seq = k_ref.shape[0]

    def head_lanes(rows, hh):
        lane = lax.broadcasted_iota(jnp.int32, (rows, LANES), 1)
        in_head = (lane >= hh * dh) & (lane < (hh + 1) * dh)
        return lane - (1 - hh) * dh, in_head

    def f_column(f_tile, hh):
        lane = lax.broadcasted_iota(jnp.int32, f_tile.shape, 1)
        return jnp.sum(jnp.where(lane == F_COL0 + 2 * hp + hh, f_tile, 0.0), axis=-1, keepdims=True)

    def f_terms(f_tile, hh):
        return [t.astype(F32) for t in _split3_bf16(f_column(f_tile, hh))]

    @pl.when(qi == 0)
    def _():
        f_all = f_ref[...]
        for hh in range(2):
            ext, in_head = head_lanes(seq, hh)
            hi, mid, lo = f_terms(f_all, hh)
            k_sp = jnp.where((ext >= 0) & (ext < 3), 1.0,
                             jnp.where(ext == 3, -hi, jnp.where(ext == 4, -mid,
                                                                jnp.where(ext == 5, -lo, 0.0))))
            kx_ref[hh] = jnp.where(in_head, k_ref[...], k_sp.astype(BF16))
            vx_ref[hh] = jnp.where(in_head, v_ref[...], jnp.where(ext == 0, 1.0, 0.0).astype(BF16))

    f_q = f_ref[pl.ds(pl.multiple_of(qi * tq, tq), tq), :]
    r_iota = lax.broadcasted_iota(jnp.int32, (tk, tk), 0)
    c_iota = lax.broadcasted_iota(jnp.int32, (tk, tk), 1)
    sub = tq // tk
    nt = (((1,), (1,)), ((), ()))
    qx = []
    for hh in range(2):
        ext, in_head = head_lanes(tq, hh)
        hi, mid, lo = f_terms(f_q, hh)
        q_sp = jnp.where(ext == 0, hi, jnp.where(ext == 1, mid, jnp.where(ext == 2, lo,
                         jnp.where((ext >= 3) & (ext < 6), 1.0, 0.0))))
        qx.append(jnp.where(in_head, q_ref[...] * 0.125, q_sp.astype(BF16)))

    def block(hh, j, state, diag):
        off = pl.multiple_of(j * tk, tk)
        first = 0 if diag is None else diag
        s = lax.dot_general(qx[hh][first * tk:], kx_ref[hh, pl.ds(off, tk), :], nt,
                            preferred_element_type=F32)
        new_m, probs = [], []
        for r in range(first, sub):
            s_r = s[(r - first) * tk:(r - first + 1) * tk]
            if diag is not None and r == diag:
                s_r = jnp.where(r_iota >= c_iota, s_r, -jnp.inf)
            s_max = jnp.max(s_r, axis=-1, keepdims=True)
            m_r = s_max if state[r] is None else jnp.maximum(state[r][0], s_max)
            new_m.append(m_r)
            probs.append(jnp.exp((s_r - m_r).astype(BF16)))
        pv = _dot(probs[0] if len(probs) == 1 else jnp.concatenate(probs, axis=0),
                  vx_ref[hh, pl.ds(off, tk), :])
        out = list(state)
        for r in range(first, sub):
            pv_r = pv[(r - first) * tk:(r - first + 1) * tk]
            m_r = new_m[r - first]
            out[r] = (m_r, pv_r if state[r] is None
                      else jnp.exp(state[r][0] - m_r) * state[r][1] + pv_r)
        return out

    st0 = st1 = [None] * sub
    for dg in range(sub):
        st0 = block(0, qi * sub + dg, st0, dg)
        st1 = block(1, qi * sub + dg, st1, dg)

    def body(j, carry):
        st0, st1 = carry
        return block(0, j, st0, None), block(1, j, st1, None)

    st0, st1 = lax.fori_loop(0, qi * sub, body, (st0, st1))
    a0 = jnp.concatenate([acc for _, acc in st0], axis=0) if sub > 1 else st0[0][1]
    a1 = jnp.concatenate([acc for _, acc in st1], axis=0) if sub > 1 else st1[0][1]
    out = None
    for hh, acc in ((0, a0), (1, a1)):
        ext, in_head = head_lanes(tq, hh)
        denom = jnp.sum(jnp.where(ext == 0, acc, 0.0), axis=-1, keepdims=True)
        o = acc / denom
        out = o if out is None else jnp.where(in_head, o, out)
    o_ref[...] = out.astype(BF16)


def _attn_call(big3, fcol):
    bsz, seq, _ = big3.shape
    tq = min(2048, seq)
    tk = min(512, tq)
    nq = seq // tq
    pairs = ATT_HEADS // 2
    kb, vb = COL_K // LANES, COL_V // LANES
    return pl.pallas_call(
        functools.partial(_attn_kernel, tq=tq, tk=tk),
        grid=(bsz, pairs, nq),
        in_specs=[
            pl.BlockSpec((None, tq, LANES), lambda b, h, i: (b, i, h)),
            pl.BlockSpec((None, seq, LANES), lambda b, h, i: (b, 0, kb + h)),
            pl.BlockSpec((None, seq, LANES), lambda b, h, i: (b, 0, vb + h)),
            pl.BlockSpec((seq, SMALL_W), lambda b, h, i: (b, 0)),
        ],
        out_specs=pl.BlockSpec((None, tq, LANES), lambda b, h, i: (b, i, h)),
        out_shape=jax.ShapeDtypeStruct((bsz, seq, ATT_WIDTH), BF16),
        scratch_shapes=[pltpu.VMEM((2, seq, LANES), BF16), pltpu.VMEM((2, seq, LANES), BF16)],
        compiler_params=_cparams(("arbitrary", "arbitrary", "arbitrary")),
        name="fox_attention",
    )(big3, big3, big3, fcol)


SSD_CONV_HALO = 8


def _ssd_kernel(z_ref, xbc_ref, dt_ref, cw_ref, cb_ref, dtb_ref, alog_ref, dsk_ref, nw_ref,
                y_ref, ext_ref, state_ref):
    @pl.when(pl.program_id(1) == 0)
    def _():
        ext_ref[0:SSD_CONV_HALO, :] = jnp.zeros((SSD_CONV_HALO, SSM_CONV_DIM), F32)
        state_ref[...] = jnp.zeros_like(state_ref)

    for ch in range(z_ref.shape[0] // SSD_CHUNK):
        rows = pl.ds(ch * SSD_CHUNK, SSD_CHUNK)
        _ssd_chunk(z_ref.at[rows, :], xbc_ref.at[rows, :], dt_ref.at[rows, :], cw_ref, cb_ref, dtb_ref,
                   alog_ref, dsk_ref, nw_ref, y_ref.at[rows, :], ext_ref, state_ref)


def _ssd_chunk(z_ref, xbc_ref, dt_ref, cw_ref, cb_ref, dtb_ref, alog_ref, dsk_ref, nw_ref,
               y_ref, ext_ref, state_ref):
    L = SSD_CHUNK
    P = SSM_HEAD_DIM
    GW = SSM_GROUP_WIDTH
    NS = SSM_STATE
    halo = SSD_CONV_HALO

    ext_ref[halo:halo + L, :] = xbc_ref[...].astype(F32)
    ext = ext_ref[...]
    conv = cb_ref[...] + cw_ref[SSM_CONV - 1:SSM_CONV, :] * ext[halo:, :]
    for lag in range(1, SSM_CONV):
        j = SSM_CONV - 1 - lag
        conv = conv + cw_ref[j:j + 1, :] * pltpu.roll(ext, lag, 0)[halo:, :]
    ext_ref[0:halo, :] = ext[L:L + halo, :]
    xc = _silu(conv)

    rows = lax.broadcasted_iota(jnp.int32, (L, LANES), 0)
    lane = lax.broadcasted_iota(jnp.int32, (L, LANES), 1)
    lo_half = lane < P
    dt = _softplus(dt_ref[...] + dtb_ref[...])
    a_cs = _cumsum_rows(dt * (-jnp.exp(alog_ref[...])), rows)
    a_t = jnp.concatenate([a_cs, a_cs], axis=0).T
    causal = rows >= jnp.where(lo_half, lane, lane - P)
    blk = (lax.broadcasted_iota(jnp.int32, (2 * L, LANES), 0) < L) == \
          (lax.broadcasted_iota(jnp.int32, (2 * L, LANES), 1) < P)

    def pair_cols(mat, h0):
        r = mat.shape[0]
        c0 = jnp.broadcast_to(mat[:, h0:h0 + 1], (r, LANES))
        c1 = jnp.broadcast_to(mat[:, h0 + 1:h0 + 2], (r, LANES))
        return jnp.where(lo_half[:r], c0, c1)

    def pair_rows(mat_t, h0):
        return jnp.where(lo_half[:1], mat_t[h0:h0 + 1, :], mat_t[h0 + 1:h0 + 2, :])

    for g in range(SSM_GROUPS):
        bm = xc[:, SSM_INNER + g * NS:SSM_INNER + (g + 1) * NS]
        cm = xc[:, SSM_INNER + SSM_GROUPS * NS + g * NS:SSM_INNER + SSM_GROUPS * NS + (g + 1) * NS]
        cm_b = cm.astype(BF16)
        bm_t2 = jnp.concatenate([bm, bm], axis=0).T.astype(BF16)
        cb2 = _dot(cm_b, bm_t2)
        st_prev = state_ref[g]
        y_off = _dot(cm_b, st_prev.astype(BF16))
        y_parts, xw_parts, cd_parts = [], [], []
        for pr in range(GW // LANES):
            h0 = g * (GW // P) + 2 * pr
            c0 = g * GW + pr * LANES
            xs_p = xc[:, c0:c0 + LANES]
            a_col = pair_cols(a_cs, h0)
            a_end = a_col[L - 1:L, :]
            seg = a_col - pair_rows(a_t, h0)
            decay = jnp.where(causal, jnp.exp(jnp.where(causal, seg, 0.0)), 0.0)
            xdt = xs_p * pair_cols(dt, h0)
            xdt2 = jnp.concatenate([xdt, xdt], axis=0)
            xdt_bd = jnp.where(blk, xdt2, 0.0).astype(BF16)
            y_d = _dot((cb2 * decay).astype(BF16), xdt_bd)
            y_o = y_off[:, pr * LANES:(pr + 1) * LANES] * jnp.exp(a_col)
            y_parts.append(y_d + y_o + dsk_ref[:, c0:c0 + LANES] * xs_p)
            xw_parts.append(xdt * jnp.exp(a_end - a_col))
            cd_parts.append(jnp.exp(a_end))
        xw = jnp.concatenate(xw_parts, axis=-1).astype(BF16)
        state_ref[g] = st_prev * jnp.concatenate(cd_parts, axis=-1) + _dot(bm_t2[:, :L], xw)
        y = jnp.concatenate(y_parts, axis=-1) * _silu(z_ref[:, g * GW:(g + 1) * GW].astype(F32))
        y = y * lax.rsqrt(jnp.mean(y * y, axis=-1, keepdims=True) + RMS_EPS)
        y_ref[:, g * GW:(g + 1) * GW] = (y * nw_ref[:, g * GW:(g + 1) * GW]).astype(BF16)


def _ssd_call(big, small, conv_w, conv_b, dtb_row, alog_row, dsk_row, nw_row, bsz, seq):
    n = big.shape[0]
    L = SSD_CHUNK * SSD_CHUNKS_PER_STEP
    nc = seq // L
    const = lambda b, c: (0, 0)
    return pl.pallas_call(
        _ssd_kernel,
        grid=(bsz, nc),
        in_specs=[
            pl.BlockSpec((L, SSM_INNER), lambda b, c: (b * nc + c, COL_Z // SSM_INNER)),
            pl.BlockSpec((L, SSM_CONV_DIM), lambda b, c: (b * nc + c, COL_XBC // SSM_CONV_DIM)),
            pl.BlockSpec((L, SMALL_W), lambda b, c: (b * nc + c, 0)),
            pl.BlockSpec((SSM_CONV, SSM_CONV_DIM), const),
            pl.BlockSpec((1, SSM_CONV_DIM), const),
            pl.BlockSpec((1, SMALL_W), const),
            pl.BlockSpec((1, SMALL_W), const),
            pl.BlockSpec((1, SSM_INNER), const),
            pl.BlockSpec((1, SSM_INNER), const),
        ],
        out_specs=pl.BlockSpec((L, SSM_INNER), lambda b, c: (b * nc + c, 0)),
        out_shape=jax.ShapeDtypeStruct((n, SSM_INNER), BF16),
        scratch_shapes=[
            pltpu.VMEM((SSD_CONV_HALO + SSD_CHUNK, SSM_CONV_DIM), F32),
            pltpu.VMEM((SSM_GROUPS, SSM_STATE, SSM_GROUP_WIDTH), F32),
        ],
        compiler_params=_cparams(("arbitrary", "arbitrary")),
        name="ssd_scan",
    )(big, big, small, conv_w, conv_b, dtb_row, alog_row, dsk_row, nw_row)


def _mixout_kernel(x_ref, ya_ref, ys_ref, u_ref, uh_ref, gate_ref, gm_ref, shf_ref, scf_ref,
                   wao_ref, wso_ref, wp_ref, ps_ref, wout_ref, lng_ref, lnb_ref, wrh_ref, wrl_ref,
                   x1_ref, h2_ref, lt_ref, ext_ref, *, tm, tiles_per_seq):
    i = pl.program_id(0)
    halo = 16
    first = (i % tiles_per_seq) == 0
    ext_ref[0:halo, :] = jnp.where(first, 0.0, uh_ref[...].astype(F32))
    u = u_ref[...].astype(F32)
    ext_ref[halo:halo + tm, :] = u
    pos = ((i % tiles_per_seq) * tm + 1 + lax.broadcasted_iota(jnp.int32, (tm, 1), 0)).astype(F32)

    y_att = _dot(ya_ref[...], wao_ref[...])
    y_ssm = _dot(ys_ref[...], wso_ref[...])
    pool_parts = []
    for g, w in enumerate(POOL_WINDOWS):
        c0 = g * POOL_GROUP_DIM
        s = ext_ref[:, c0:c0 + POOL_GROUP_DIM]
        k = 1
        while k < w:
            s = s + pltpu.roll(s, k, 0)
            k *= 2
        pooled = s[halo:, :] / jnp.minimum(pos, float(w)) - u[:, c0:c0 + POOL_GROUP_DIM]
        pool_parts.append(_dot(pooled.astype(BF16), wp_ref[g]))
    y_pool = jnp.concatenate(pool_parts, axis=-1) * ps_ref[...]

    g_att = _sigmoid(gate_ref[:, 0:D_MODEL].astype(F32))
    g_pool = _sigmoid(gate_ref[:, D_MODEL:2 * D_MODEL].astype(F32))
    g_ssm = _sigmoid(gate_ref[:, 2 * D_MODEL:3 * D_MODEL].astype(F32))
    merged = g_att * y_att + g_pool * y_pool + g_ssm * y_ssm
    mix = _dot(merged.astype(BF16), wout_ref[...])
    x1 = _layer_norm(DEEPNORM_ALPHA * x_ref[...] + gm_ref[0] * mix) * lng_ref[...] + lnb_ref[...]
    x1_ref[...] = x1
    h2 = _layer_norm(x1) * (1.0 + scf_ref[0]) + shf_ref[0]
    _store_token_major(h2_ref, h2)
    h_hi, h_lo = _split_bf16(h2)
    nt = (((1,), (1,)), ((), ()))
    lt_ref[...] = (lax.dot_general(wrh_ref[...], h_hi, nt, preferred_element_type=F32)
                   + (lax.dot_general(wrh_ref[...], h_lo, nt, preferred_element_type=F32)
                      + lax.dot_general(wrl_ref[...], h_hi, nt, preferred_element_type=F32)))


def _mixout_call(x2, y_att, y_ssm, big, gm, shf, scf, wao, wso, wp, ps_row, wout, lng, lnb,
                 wr_hi, wr_lo, seq):
    n, d = x2.shape
    tm = 512
    halo = 16
    tiles_per_seq = seq // tm
    const2 = lambda i: (0, 0)
    per_seq = lambda i: (i // tiles_per_seq, 0, 0)
    return pl.pallas_call(
        functools.partial(_mixout_kernel, tm=tm, tiles_per_seq=tiles_per_seq),
        grid=(n // tm,),
        in_specs=[
            pl.BlockSpec((tm, d), lambda i: (i, 0)),
            pl.BlockSpec((tm, ATT_WIDTH), lambda i: (i, 0)),
            pl.BlockSpec((tm, SSM_INNER), lambda i: (i, 0)),
            pl.BlockSpec((tm, POOL_WIDTH), lambda i: (i, COL_POOL // POOL_WIDTH)),
            pl.BlockSpec((halo, POOL_WIDTH),
                         lambda i: (jnp.maximum(i * (tm // halo) - 1, 0), COL_POOL // POOL_WIDTH)),
            pl.BlockSpec((tm, N_BRANCHES * d), lambda i: (i, COL_GATE // (N_BRANCHES * d))),
            pl.BlockSpec((1, 1, d), per_seq),
            pl.BlockSpec((1, 1, d), per_seq),
            pl.BlockSpec((1, 1, d), per_seq),
            pl.BlockSpec((ATT_WIDTH, d), const2),
            pl.BlockSpec((SSM_INNER, d), const2),
            pl.BlockSpec((POOL_GROUPS, POOL_GROUP_DIM, POOL_GROUP_DIM), lambda i: (0, 0, 0)),
            pl.BlockSpec((1, d), const2),
            pl.BlockSpec((d, d), const2),
            pl.BlockSpec((1, d), const2),
            pl.BlockSpec((1, d), const2),
            pl.BlockSpec((N_EXPERTS, d), const2),
            pl.BlockSpec((N_EXPERTS, d), const2),
        ],
        out_specs=[
            pl.BlockSpec((tm, d), lambda i: (i, 0)),
            pl.BlockSpec((tm * (d // LANES), LANES), lambda i: (i, 0)),
            pl.BlockSpec((N_EXPERTS, tm), lambda i: (0, i)),
        ],
        out_shape=[
            jax.ShapeDtypeStruct((n, d), F32),
            jax.ShapeDtypeStruct((n * (d // LANES), LANES), F32),
            jax.ShapeDtypeStruct((N_EXPERTS, n), F32),
        ],
        scratch_shapes=[pltpu.VMEM((halo + tm, POOL_WIDTH), F32)],
        compiler_params=_cparams(("arbitrary",)),
        name="mixer_out",
    )(x2, y_att, y_ssm, big, big, big, gm, shf, scf, wao, wso, wp, ps_row, wout, lng, lnb,
      wr_hi, wr_lo)


def _top2(vals):
    n = len(vals)
    v1 = vals[0]
    for v in vals[1:]:
        v1 = jnp.maximum(v1, v)
    i1 = jnp.full(v1.shape, n, jnp.int32)
    for j in reversed(range(n)):
        i1 = jnp.where(vals[j] == v1, j, i1)
    v2 = jnp.full(v1.shape, -jnp.inf, F32)
    for j in range(n):
        v2 = jnp.maximum(v2, jnp.where(i1 == j, -jnp.inf, vals[j]))
    i2 = jnp.full(v1.shape, n, jnp.int32)
    for j in reversed(range(n)):
        i2 = jnp.where((vals[j] == v2) & (i1 != j), j, i2)
    return v1, i1, v2, i2


def _router_kernel(lt_ref, br_ref, e_ref, w_ref, cnt_ref):
    lg = lt_ref[...]
    m = jnp.max(lg, axis=0, keepdims=True)
    ex = jnp.exp(lg - m)
    probs = ex / jnp.sum(ex, axis=0, keepdims=True)
    sel = probs + br_ref[...]
    tops = []
    for g in range(N_EXPERT_GROUPS):
        vals = [sel[g * EXPERTS_PER_GROUP + j:g * EXPERTS_PER_GROUP + j + 1, :]
                for j in range(EXPERTS_PER_GROUP)]
        tops.append(_top2(vals))
    best = tops[0][0] + tops[0][2]
    e1 = tops[0][1]
    e2 = tops[0][3]
    for g in range(1, N_EXPERT_GROUPS):
        score = tops[g][0] + tops[g][2]
        better = score > best
        best = jnp.where(better, score, best)
        e1 = jnp.where(better, tops[g][1] + g * EXPERTS_PER_GROUP, e1)
        e2 = jnp.where(better, tops[g][3] + g * EXPERTS_PER_GROUP, e2)
    p1 = jnp.zeros_like(best)
    p2 = jnp.zeros_like(best)
    for e in range(N_EXPERTS):
        p1 = jnp.where(e1 == e, probs[e:e + 1, :], p1)
        p2 = jnp.where(e2 == e, probs[e:e + 1, :], p2)
    tot = p1 + p2
    e_ref[...] = jnp.concatenate([e1, e2], axis=0)
    w_ref[...] = jnp.concatenate([p1 / tot, p2 / tot], axis=0)

    @pl.when(pl.program_id(0) == 0)
    def _():
        cnt_ref[...] = jnp.zeros_like(cnt_ref)

    e_rows = lax.broadcasted_iota(jnp.int32, lg.shape, 0)
    hits = ((e_rows == e1) | (e_rows == e2)).astype(jnp.int32)
    cnt_ref[...] += jnp.sum(hits, axis=1, keepdims=True)


def _router_call(logits_t, br_col):
    e, n = logits_t.shape
    t = min(2048, n)
    return pl.pallas_call(
        _router_kernel,
        grid=(n // t,),
        in_specs=[
            pl.BlockSpec((e, t), lambda i: (0, i)),
            pl.BlockSpec((e, 1), lambda i: (0, 0)),
        ],
        out_specs=[
            pl.BlockSpec((TOP_K, t), lambda i: (0, i)),
            pl.BlockSpec((TOP_K, t), lambda i: (0, i)),
            pl.BlockSpec((e, LANES), lambda i: (0, 0)),
        ],
        out_shape=[
            jax.ShapeDtypeStruct((TOP_K, n), jnp.int32),
            jax.ShapeDtypeStruct((TOP_K, n), F32),
            jax.ShapeDtypeStruct((e, LANES), jnp.int32),
        ],
        compiler_params=_cparams(("arbitrary",)),
        name="router_top2",
    )(logits_t, br_col)


def _dispatch_kernel(dest_ref, h_ref, zero_hbm, x_hbm, sem):
    del zero_hbm
    c = D_MODEL // LANES
    t = h_ref.shape[0] // c
    base = pl.program_id(0) * (t * TOP_K)
    for k in range(t):
        for j in range(TOP_K):
            row = pl.multiple_of(dest_ref[base + k * TOP_K + j], c)
            pltpu.make_async_copy(h_ref.at[pl.ds(k * c, c), :], x_hbm.at[pl.ds(row, c), :],
                                  sem).start(priority=j % DMA_PRIORITIES)
    for j in range(TOP_K):
        pltpu.make_async_copy(h_ref, x_hbm.at[pl.ds(0, t * c), :], sem).wait()


def _dispatch_call(dest_rows, h2_tiles, p):
    rows, _ = h2_tiles.shape
    c = D_MODEL // LANES
    t = min(256, rows // c)
    grid_spec = pltpu.PrefetchScalarGridSpec(
        num_scalar_prefetch=1,
        grid=(rows // (t * c),),
        in_specs=[pl.BlockSpec((t * c, LANES), lambda i, dest: (i, 0)),
                  pl.BlockSpec(memory_space=pl.ANY)],
        out_specs=pl.BlockSpec(memory_space=pl.ANY),
        scratch_shapes=[pltpu.SemaphoreType.DMA(())],
    )
    return pl.pallas_call(
        _dispatch_kernel,
        grid_spec=grid_spec,
        out_shape=jax.ShapeDtypeStruct((p * c, LANES), F32),
        input_output_aliases={2: 0},
        compiler_params=_cparams(("arbitrary",)),
        name="moe_dispatch",
    )(dest_rows, h2_tiles, jnp.zeros((p * c, LANES), F32))


def _expert_kernel(dst_ref, bexp_ref,
                   w1a_ref, w3a_ref, w2a_ref, swa_ref, xa, w1b_ref, w3b_ref, w2b_ref, swb_ref, xb,
                   y_hbm, ya, yb, w13a_bf, w2a_bf, w13b_bf, w2b_bf, ssem):
    i = pl.program_id(0)
    last = pl.num_programs(0) - 1
    c = w1a_ref.shape[1] // LANES

    def start_scatter(blk, ybuf, sem):
        base = (blk + 1) * MOE_BLOCK
        for k in range(MOE_BLOCK):
            row = pl.multiple_of(dst_ref[base + k], c)
            pltpu.make_async_copy(ybuf.at[pl.ds(k * c, c), :], y_hbm.at[pl.ds(row, c), :],
                                  sem).start(priority=k % DMA_PRIORITIES)

    def wait_scatter(ybuf, sem):
        pltpu.make_async_copy(ybuf, y_hbm.at[pl.ds(0, MOE_BLOCK * c), :], sem).wait()

    def ffn(xbuf, w_bf, sw_ref, ybuf):
        w13_bf, w2_bf = w_bf
        xv = _load_token_major(xbuf, MOE_BLOCK, c).astype(BF16)
        hid = _silu(_dot(xv, w13_bf[0])) * _dot(xv, w13_bf[1])
        _store_token_major(ybuf, _dot(hid.astype(BF16), w2_bf[...]) * sw_ref[...])

    def refresh_weights(parity, w1_ref, w3_ref, w2_ref, w_bf):
        w13_bf, w2_bf = w_bf
        blk = 2 * i + parity
        changed = (i == 0) | (bexp_ref[blk] != bexp_ref[jnp.maximum(blk - 2, 0)])

        @pl.when(changed)
        def _():
            w13_bf[0] = w1_ref[0].astype(BF16)
            w13_bf[1] = w3_ref[0].astype(BF16)
            w2_bf[...] = w2_ref[0].astype(BF16)

    @pl.when(i == 0)
    def _():
        yb[...] = jnp.zeros_like(yb)

    @pl.when(i > 0)
    def _():
        wait_scatter(ya, ssem.at[0])

    refresh_weights(0, w1a_ref, w3a_ref, w2a_ref, (w13a_bf, w2a_bf))
    refresh_weights(1, w1b_ref, w3b_ref, w2b_ref, (w13b_bf, w2b_bf))

    start_scatter(2 * i - 1, yb, ssem.at[1])
    ffn(xa, (w13a_bf, w2a_bf), swa_ref, ya)

    wait_scatter(yb, ssem.at[1])
    start_scatter(2 * i, ya, ssem.at[0])
    ffn(xb, (w13b_bf, w2b_bf), swb_ref, yb)

    @pl.when(i == last)
    def _():
        start_scatter(2 * i + 1, yb, ssem.at[1])
        wait_scatter(ya, ssem.at[0])
        wait_scatter(yb, ssem.at[1])


def _expert_call(dst_ext, blk_expert, x_sorted, w1, w3, w2, layer, slot_w):
    d = w1.shape[2]
    c = d // LANES
    p = slot_w.shape[0]
    nblk = p // MOE_BLOCK
    assert nblk % 2 == 0 and dst_ext.shape[0] == p + MOE_BLOCK

    def weight_spec(shape, parity):
        return pl.BlockSpec(shape, lambda i, dst, be: (layer, be[2 * i + parity], 0, 0))

    def block_specs(parity):
        return [weight_spec((None, 1, d, EXPERT_DFF), parity), weight_spec((None, 1, d, EXPERT_DFF), parity),
                weight_spec((None, 1, EXPERT_DFF, d), parity),
                pl.BlockSpec((MOE_BLOCK, 1), lambda i, dst, be: (2 * i + parity, 0)),
                pl.BlockSpec((MOE_BLOCK * c, LANES), lambda i, dst, be: (2 * i + parity, 0))]

    grid_spec = pltpu.PrefetchScalarGridSpec(
        num_scalar_prefetch=2,
        grid=(nblk // 2,),
        in_specs=block_specs(0) + block_specs(1),
        out_specs=pl.BlockSpec(memory_space=pl.ANY),
        scratch_shapes=[pltpu.VMEM((MOE_BLOCK * c, LANES), F32)] * 2
        + [pltpu.VMEM((2, d, EXPERT_DFF), BF16), pltpu.VMEM((EXPERT_DFF, d), BF16)] * 2
        + [pltpu.SemaphoreType.DMA((2,))],
    )
    return pl.pallas_call(
        _expert_kernel,
        grid_spec=grid_spec,
        out_shape=jax.ShapeDtypeStruct(((p + MOE_BLOCK) * c, LANES), F32),
        compiler_params=_cparams(("arbitrary",)),
        name="moe_experts",
    )(dst_ext, blk_expert, w1, w3, w2, slot_w, x_sorted, w1, w3, w2, slot_w, x_sorted)


def _combine_kernel(x_ref, y_ref, gf_ref, lng_ref, lnb_ref, o_ref):
    tm, d = x_ref.shape
    c = d // LANES
    ffn = _load_token_major(y_ref, tm, c, 0, TOP_K) + _load_token_major(y_ref, tm, c, 1, TOP_K)
    o_ref[...] = _layer_norm(DEEPNORM_ALPHA * x_ref[...] + gf_ref[0] * ffn) * lng_ref[...] + lnb_ref[...]


def _combine_call(x1, y_pairs, gf, lng, lnb, seq):
    n, d = x1.shape
    tm = 512
    tiles_per_seq = seq // tm
    return pl.pallas_call(
        _combine_kernel,
        grid=(n // tm,),
        in_specs=[
            pl.BlockSpec((tm, d), lambda i: (i, 0)),
            pl.BlockSpec((tm * TOP_K * (d // LANES), LANES), lambda i: (i, 0)),
            pl.BlockSpec((1, 1, d), lambda i: (i // tiles_per_seq, 0, 0)),
            pl.BlockSpec((1, d), lambda i: (0, 0)),
            pl.BlockSpec((1, d), lambda i: (0, 0)),
        ],
        out_specs=pl.BlockSpec((tm, d), lambda i: (i, 0)),
        out_shape=jax.ShapeDtypeStruct((n, d), F32),
        compiler_params=_cparams(("arbitrary",)),
        name="moe_combine",
    )(x1, y_pairs, gf, lng, lnb)


def _slot_tables(e_idx, gate_w, counts, n, tile_rows):
    nk = n * TOP_K
    e_flat = e_idx.reshape(nk)
    w_flat = gate_w.reshape(nk)
    order = jnp.argsort(e_flat).astype(jnp.int32)
    starts = jnp.cumsum(counts) - counts
    padded = ((counts + MOE_BLOCK - 1) // MOE_BLOCK) * MOE_BLOCK
    pends = jnp.cumsum(padded)
    pstarts = pends - padded
    p = nk + N_EXPERTS * MOE_BLOCK
    nblk = p // MOE_BLOCK
    blk_start = jnp.arange(nblk, dtype=jnp.int32) * MOE_BLOCK
    blk_expert = jnp.minimum(jnp.sum(pends[None, :] <= blk_start[:, None], axis=1),
                             N_EXPERTS - 1).astype(jnp.int32)
    per_slot = lambda tbl: jnp.repeat(tbl[blk_expert], MOE_BLOCK)
    rank = jnp.arange(p, dtype=jnp.int32) - per_slot(pstarts)
    is_pad = rank >= per_slot(counts)
    slot_flat = order[jnp.minimum(per_slot(starts) + rank, nk - 1)]
    slot_w = jnp.where(is_pad, 0.0, w_flat[slot_flat])
    pads_before = per_slot(pstarts - starts) + rank - per_slot(counts)
    slot_dst = jnp.where(is_pad, nk + pads_before, slot_flat)
    dst_ext = jnp.concatenate([p + jnp.arange(MOE_BLOCK, dtype=jnp.int32), slot_dst])
    pos = jnp.argsort(order).astype(jnp.int32)
    dest = pos + (pstarts - starts)[e_flat]
    return dest * tile_rows, dst_ext * tile_rows, blk_expert, slot_w.reshape(p, 1)


def kernel(x, c, w_in, b_forget, w_attn_o, w_pool, pool_scale, conv_w, conv_b, dt_bias, a_log,
           d_skip, ssm_norm_w, w_ssm_o, w_out, w_ada, b_ada, ln_mix_g, ln_mix_b, ln_ffn_g,
           ln_ffn_b, w_router, b_router, w_exp_gate, w_exp_up, w_exp_down):
    bsz, seq, d = x.shape
    n = bsz * seq
    mod = _ada_call(c, w_ada, b_ada)
    wr_hi, wr_lo = _split_bf16(w_router.T)
    br_col = b_router.reshape(N_EXPERTS, 1)
    x2 = x.reshape(n, d)
    w_big = _w_in_call(w_in)
    o_f = 3 * ATT_WIDTH
    o_dt = o_f + ATT_HEADS + POOL_WIDTH + SSM_INNER + SSM_CONV_DIM
    w_small = jnp.concatenate(
        [w_in[:, :, o_dt:o_dt + SSM_HEADS], w_in[:, :, o_f:o_f + ATT_HEADS],
         jnp.zeros((DEPTH, d, SMALL_W - SSM_HEADS - ATT_HEADS), F32)], axis=2)
    ws_hi, ws_lo = _split_bf16(w_small)
    for l in range(DEPTH):
        sh_m, sc_m, g_m, sh_f, sc_f, g_f = (mod[l, :, i * d:(i + 1) * d].reshape(bsz, 1, d)
                                            for i in range(6))
        big, small = _inproj_call(x2, sh_m, sc_m, w_big, ws_hi, ws_lo, l, seq)

        bf_row = jnp.zeros((1, SMALL_W), F32).at[0, F_COL0:F_COL0 + ATT_HEADS].set(b_forget[l])
        fcol = _fcum_call(small, bf_row, bsz, seq)
        y_att = _attn_call(big.reshape(bsz, seq, BIG_W), fcol).reshape(n, ATT_WIDTH)

        pad_heads = jnp.zeros((SMALL_W - SSM_HEADS,), F32)
        dtb_row = jnp.concatenate([dt_bias[l], pad_heads]).reshape(1, SMALL_W)
        alog_row = jnp.concatenate([a_log[l], pad_heads]).reshape(1, SMALL_W)
        dsk_row = jnp.repeat(d_skip[l], SSM_HEAD_DIM).reshape(1, SSM_INNER)
        y_ssm = _ssd_call(big, small, conv_w[l], conv_b[l].reshape(1, SSM_CONV_DIM), dtb_row,
                          alog_row, dsk_row, ssm_norm_w[l].reshape(1, SSM_INNER), bsz, seq)

        x1, h2, logits_t = _mixout_call(
            x2, y_att, y_ssm, big, g_m, sh_f, sc_f, w_attn_o[l].astype(BF16),
            w_ssm_o[l].astype(BF16), w_pool[l].astype(BF16), pool_scale[l].reshape(1, d),
            w_out[l].astype(BF16), ln_mix_g[l].reshape(1, d), ln_mix_b[l].reshape(1, d),
            wr_hi, wr_lo, seq)

        e_idx_t, gate_t, counts = _router_call(logits_t, br_col)
        dest, dst_ext, blk_expert, slot_w = _slot_tables(e_idx_t.T, gate_t.T, counts[:, 0], n,
                                                         d // LANES)
        x_sorted = _dispatch_call(dest, h2, slot_w.shape[0])
        y_pairs = _expert_call(dst_ext, blk_expert, x_sorted,
                               w_exp_gate, w_exp_up, w_exp_down, l, slot_w)
        x2 = _combine_call(x1, y_pairs, g_f, ln_ffn_g[l].reshape(1, d), ln_ffn_b[l].reshape(1, d), seq)
    return x2.reshape(bsz, seq, d)
```
